```python
import jax, jax.numpy as jnp
from jax import lax
import numpy as np

D_MODEL = 2048
BATCH = 4
SEQ = 4096
DEPTH = 2

CHUNK = 64
N_EVEN = (DEPTH + 1) // 2
N_ODD = DEPTH // 2

A_HEADS = 8
A_DK = 128
A_DV = 128
A_CONV = 4
B_GROUPS = 8
B_DG = 128
B_BLOCK = 128
A_QK = A_HEADS * A_DK
A_V = A_HEADS * A_DV
B_W = B_GROUPS * B_DG
A_QKV = 2 * A_QK + A_V
IN_COLS = A_QKV + A_V + 2 * A_HEADS + 2 * B_W
MIX_W = A_V + B_W
C_WIDTH = 31
D_FF = 4 * D_MODEL
EPS = 1e-6

kernel_name = "hybrid_deltanet_gmlp_conformer_block"


def rmsnorm(x, g):
    xf = x.astype(jnp.float32)
    y = xf * lax.rsqrt(jnp.mean(xf * xf, axis=-1, keepdims=True) + EPS)
    return (y * g.astype(jnp.float32)).astype(x.dtype)


def layernorm(x, g, b):
    xf = x.astype(jnp.float32)
    mu = jnp.mean(xf, axis=-1, keepdims=True)
    xc = xf - mu
    y = xc * lax.rsqrt(jnp.mean(xc * xc, axis=-1, keepdims=True) + EPS)
    return (y * g.astype(jnp.float32) + b.astype(jnp.float32)).astype(x.dtype)


def l2norm(x):
    return x * lax.rsqrt(jnp.sum(x * x, axis=-1, keepdims=True) + EPS)


def causal_dwconv(x, w):
    k = w.shape[0]
    return lax.conv_general_dilated(
        x, w[:, None, :].astype(x.dtype), window_strides=(1,), padding=[(k - 1, 0)],
        dimension_numbers=("NWC", "WIO", "NWC"), feature_group_count=x.shape[-1])


def gated_delta_rule(q, k, v, g, beta):
    bsz, seq, heads, dk = q.shape
    dv = v.shape[-1]
    n = seq // CHUNK

    def chunks(t):
        return jnp.moveaxis(t.astype(jnp.float32).reshape(bsz, n, CHUNK, heads, -1), 3, 1)

    q = chunks(q) * (dk ** -0.5)
    k = chunks(k)
    v = chunks(v)
    g = jnp.moveaxis(g.astype(jnp.float32).reshape(bsz, n, CHUNK, heads), 3, 1)
    beta = jnp.moveaxis(beta.astype(jnp.float32).reshape(bsz, n, CHUNK, heads), 3, 1)
    g = jnp.cumsum(g, axis=-1)

    tri = jnp.tril(jnp.ones((CHUNK, CHUNK), dtype=bool))
    strict = jnp.tril(jnp.ones((CHUNK, CHUNK), dtype=bool), -1)
    eye = jnp.eye(CHUNK, dtype=jnp.float32)
    decay = jnp.exp(jnp.where(tri, g[..., :, None] - g[..., None, :], -jnp.inf))

    k_beta = k * beta[..., None]
    v_beta = v * beta[..., None]
    a = jnp.einsum("bhnid,bhnjd->bhnij", k_beta, k) * decay
    a = jnp.where(strict, a, 0.0) + eye
    rhs = jnp.concatenate([v_beta, k_beta * jnp.exp(g)[..., None]], axis=-1)
    sol = lax.linalg.triangular_solve(a, rhs, left_side=True, lower=True, unit_diagonal=True)
    u = sol[..., :dv]
    w = sol[..., dv:]

    attn = jnp.einsum("bhnid,bhnjd->bhnij", q, k) * decay
    q_dec = q * jnp.exp(g)[..., None]
    k_tail = k * jnp.exp(g[..., -1:] - g)[..., None]
    g_last = jnp.exp(g[..., -1])

    xs = tuple(jnp.moveaxis(t, 2, 0) for t in (attn, q_dec, k_tail, u, w, g_last))

    def step(state, inp):
        attn_c, qd_c, kt_c, u_c, w_c, gl_c = inp
        v_new = u_c - jnp.einsum("bhck,bhkv->bhcv", w_c, state)
        o_c = (jnp.einsum("bhck,bhkv->bhcv", qd_c, state)
               + jnp.einsum("bhij,bhjv->bhiv", attn_c, v_new))
        state = state * gl_c[..., None, None] + jnp.einsum("bhck,bhcv->bhkv", kt_c, v_new)
        return state, o_c

    s0 = jnp.zeros((bsz, heads, dk, dv), jnp.float32)
    _, o = lax.scan(step, s0, xs)
    return jnp.transpose(o, (1, 0, 3, 2, 4)).reshape(bsz, seq, heads, dv)


def even_mixer(h, w_in, conv_w, a_log, dt_bias, o_norm, ln_g, ln_b, w_s, b_s, w_out):
    bsz, seq, _ = h.shape
    proj = h @ w_in.astype(h.dtype)
    cuts = [int(c) for c in np.cumsum([A_QKV, A_V, A_HEADS, A_HEADS, B_W])]
    qkv, z, b_raw, a_raw, u_raw, v_raw = jnp.split(proj, cuts, axis=-1)

    qkv = jax.nn.silu(causal_dwconv(qkv, conv_w))
    q, k, v = jnp.split(qkv, [A_QK, 2 * A_QK], axis=-1)
    q = l2norm(q.astype(jnp.float32).reshape(bsz, seq, A_HEADS, A_DK))
    k = l2norm(k.astype(jnp.float32).reshape(bsz, seq, A_HEADS, A_DK))
    v = v.reshape(bsz, seq, A_HEADS, A_DV)
    beta = jax.nn.sigmoid(b_raw.astype(jnp.float32))
    g = -jnp.exp(a_log.astype(jnp.float32)) * jax.nn.softplus(
        a_raw.astype(jnp.float32) + dt_bias.astype(jnp.float32))
    o = gated_delta_rule(q, k, v, g, beta)
    o = rmsnorm(o, o_norm) * jax.nn.silu(z.astype(jnp.float32).reshape(bsz, seq, A_HEADS, A_DV))
    out_a = o.reshape(bsz, seq, A_V).astype(h.dtype)

    u = jax.nn.gelu(u_raw, approximate=False)
    vg = jax.nn.gelu(v_raw, approximate=False).reshape(bsz, seq, B_GROUPS, B_DG)
    vg = layernorm(vg, ln_g.reshape(B_GROUPS, B_DG), ln_b.reshape(B_GROUPS, B_DG))
    vg = vg.reshape(bsz, seq // B_BLOCK, B_BLOCK, B_GROUPS, B_DG)
    pos = np.arange(B_BLOCK)
    mask = (pos[None, :] // CHUNK) <= (pos[:, None] // CHUNK)
    w_masked = jnp.where(mask, w_s, 0.0).astype(h.dtype)
    mixed = jnp.einsum("gij,bmjgc->bmigc", w_masked, vg) + b_s.T.astype(h.dtype)[None, None, :, :, None]
    out_b = u * mixed.reshape(bsz, seq, B_W)

    return jnp.concatenate([out_a, out_b], axis=-1) @ w_out.astype(h.dtype)


def conformer_conv(h, pw1, pw1_b, dw, dw_b, ln_g, ln_b, pw2, pw2_b):
    z = h @ pw1.astype(h.dtype) + pw1_b.astype(h.dtype)
    z = z[..., :D_MODEL] * jax.nn.sigmoid(z[..., D_MODEL:])
    z = causal_dwconv(z, dw) + dw_b.astype(h.dtype)
    z = jax.nn.silu(layernorm(z, ln_g, ln_b))
    return z @ pw2.astype(h.dtype) + pw2_b.astype(h.dtype)


def sq_relu_mlp(h, w1, w2):
    a = jax.nn.relu(h @ w1.astype(h.dtype))
    return (a * a) @ w2.astype(h.dtype)


def setup_inputs(seed: int = 0) -> dict:
    key = jax.random.key(seed)
    ks = iter(jax.random.split(key, 40))

    def nrm(shape, scale):
        return jax.random.normal(next(ks), shape, jnp.float32) * scale

    def gain(shape):
        return 1.0 + nrm(shape, 0.05)

    dt = jnp.exp(jax.random.uniform(next(ks), (N_EVEN, A_HEADS), jnp.float32,
                                    np.log(1e-3), np.log(1e-1)))
    return {
        "x": nrm((BATCH, SEQ, D_MODEL), 1.0),
        "e_norm": gain((N_EVEN, D_MODEL)),
        "e_w_in": nrm((N_EVEN, D_MODEL, IN_COLS), D_MODEL ** -0.5),
        "e_conv_w": nrm((N_EVEN, A_CONV, A_QKV), A_CONV ** -0.5),
        "e_a_log": jnp.log(jax.random.uniform(next(ks), (N_EVEN, A_HEADS), jnp.float32, 1.0, 16.0)),
        "e_dt_bias": dt + jnp.log(-jnp.expm1(-dt)),
        "e_o_norm": gain((N_EVEN, A_DV)),
        "e_ln_g": gain((N_EVEN, B_W)),
        "e_ln_b": nrm((N_EVEN, B_W), 0.02),
        "e_w_s": nrm((N_EVEN, B_GROUPS, B_BLOCK, B_BLOCK), B_BLOCK ** -0.5),
        "e_b_s": gain((N_EVEN, B_GROUPS, B_BLOCK)),
        "e_w_out": nrm((N_EVEN, MIX_W, D_MODEL), MIX_W ** -0.5),
        "o_norm": gain((N_ODD, D_MODEL)),
        "o_pw1": nrm((N_ODD, D_MODEL, 2 * D_MODEL), D_MODEL ** -0.5),
        "o_pw1_b": nrm((N_ODD, 2 * D_MODEL), 0.02),
        "o_dw": nrm((N_ODD, C_WIDTH, D_MODEL), C_WIDTH ** -0.5),
        "o_dw_b": nrm((N_ODD, D_MODEL), 0.02),
        "o_ln_g": gain((N_ODD, D_MODEL)),
        "o_ln_b": nrm((N_ODD, D_MODEL), 0.02),
        "o_pw2": nrm((N_ODD, D_MODEL, D_MODEL), D_MODEL ** -0.5),
        "o_pw2_b": nrm((N_ODD, D_MODEL), 0.02),
        "f_norm": gain((DEPTH, D_MODEL)),
        "f_w1": nrm((DEPTH, D_MODEL, D_FF), D_MODEL ** -0.5),
        "f_w2": nrm((DEPTH, D_FF, D_MODEL), D_FF ** -0.5),
        "final_norm": gain((D_MODEL,)),
    }


def reference(x, e_norm, e_w_in, e_conv_w, e_a_log, e_dt_bias, e_o_norm, e_ln_g, e_ln_b,
              e_w_s, e_b_s, e_w_out, o_norm, o_pw1, o_pw1_b, o_dw, o_dw_b, o_ln_g, o_ln_b,
              o_pw2, o_pw2_b, f_norm, f_w1, f_w2, final_norm):
    for layer in range(DEPTH):
        i = layer // 2
        if layer % 2 == 0:
            x = x + even_mixer(rmsnorm(x, e_norm[i]), e_w_in[i], e_conv_w[i], e_a_log[i],
                               e_dt_bias[i], e_o_norm[i], e_ln_g[i], e_ln_b[i], e_w_s[i],
                               e_b_s[i], e_w_out[i])
        else:
            x = x + conformer_conv(rmsnorm(x, o_norm[i]), o_pw1[i], o_pw1_b[i], o_dw[i],
                                   o_dw_b[i], o_ln_g[i], o_ln_b[i], o_pw2[i], o_pw2_b[i])
        x = x + sq_relu_mlp(rmsnorm(x, f_norm[layer]), f_w1[layer], f_w2[layer])
    return rmsnorm(x, final_norm)
```

```python
import functools

import jax
import jax.numpy as jnp
from jax import lax
from jax.experimental import pallas as pl
from jax.experimental.pallas import tpu as pltpu

F32 = jnp.float32
BF16 = jnp.bfloat16
EPS = 1e-6
CHUNK = 64
LANES = 128
HALO = 32
VMEM_LIMIT = 56 * 1024 * 1024
HI = lax.Precision.HIGHEST


def _cparams(n_axes):
    return pltpu.CompilerParams(dimension_semantics=("arbitrary",) * n_axes,
                                vmem_limit_bytes=VMEM_LIMIT)


def _bdot(a, b):
    return jnp.dot(a.astype(BF16), b.astype(BF16), preferred_element_type=F32)


def _dot_nt(a, b):
    return lax.dot_general(a.astype(BF16), b.astype(BF16), (((1,), (1,)), ((), ())),
                           preferred_element_type=F32)


def _dot_tn(a, b):
    return lax.dot_general(a.astype(BF16), b.astype(BF16), (((0,), (0,)), ((), ())),
                           preferred_element_type=F32)


def _rms(x, g):
    return x * lax.rsqrt(jnp.mean(x * x, axis=-1, keepdims=True) + EPS) * g


def _silu(x):
    return x * jax.nn.sigmoid(x)


def _gelu(x):
    return 0.5 * x * (1.0 + lax.erf(x * (2.0 ** -0.5)))


def _inproj_kernel(x_ref, g_ref, w_ref, wba_ref, p_ref, ba_ref, xn_ref):
    @pl.when(pl.program_id(1) == 0)
    def _():
        xn = _rms(x_ref[...], g_ref[...]).astype(BF16)
        xn_ref[...] = xn
        ba_ref[...] = jnp.dot(xn, wba_ref[...], preferred_element_type=F32)

    p_ref[...] = jnp.dot(xn_ref[...], w_ref[...], preferred_element_type=F32)


def _inproj(x2, gain, w_main, w_ba, bm, bn):
    m, d = x2.shape
    n = w_main.shape[1]
    return pl.pallas_call(
        _inproj_kernel,
        grid=(m // bm, n // bn),
        in_specs=[
            pl.BlockSpec((bm, d), lambda i, j: (i, 0)),
            pl.BlockSpec((1, d), lambda i, j: (0, 0)),
            pl.BlockSpec((d, bn), lambda i, j: (0, j)),
            pl.BlockSpec((d, LANES), lambda i, j: (0, 0)),
        ],
        out_specs=[
            pl.BlockSpec((bm, bn), lambda i, j: (i, j)),
            pl.BlockSpec((bm, LANES), lambda i, j: (i, 0)),
        ],
        out_shape=[jax.ShapeDtypeStruct((m, n), F32), jax.ShapeDtypeStruct((m, LANES), F32)],
        scratch_shapes=[pltpu.VMEM((bm, d), BF16)],
        compiler_params=_cparams(2),
        name="inproj",
    )(x2, gain.reshape(1, d), w_main, w_ba)


def _gdn_kernel(alog_ref, dtb_ref, q_ref, k_ref, v_ref, z_ref, ba_ref, cwq_ref, cwk_ref, cwv_ref,
                onorm_ref, o_ref,
                state_ref, qpad, kpad, vpad, qs, ks, vs, beta_s, g_s, u_s, wq_s, kt_s, attn_s,
                gl_s, o_s, *, heads, ts):
    h = pl.program_id(1)
    seg = pl.program_id(2)
    nch = ts // CHUNK
    dk = q_ref.shape[-1]

    @pl.when(seg == 0)
    def _():
        state_ref[...] = jnp.zeros_like(state_ref)
        for pad in (qpad, kpad, vpad):
            pad[0:8, :] = jnp.zeros((8, dk), F32)

    def conv(raw_ref, pad, cw_ref):
        pad[8:8 + ts, :] = raw_ref[0]
        acc = pad[5:5 + ts, :] * cw_ref[0:1, :]
        for j in range(1, 4):
            acc = acc + pad[5 + j:5 + j + ts, :] * cw_ref[j:j + 1, :]
        pad[0:8, :] = pad[ts:ts + 8, :]
        return _silu(acc)

    def l2n(t):
        return t * lax.rsqrt(jnp.sum(t * t, axis=-1, keepdims=True) + EPS)

    qs[...] = l2n(conv(q_ref, qpad, cwq_ref)) * (dk ** -0.5)
    ks[...] = l2n(conv(k_ref, kpad, cwk_ref))
    vs[...] = conv(v_ref, vpad, cwv_ref)

    ba = ba_ref[0]
    row = lax.broadcasted_iota(jnp.int32, (LANES, LANES), 0)
    sel_b = (row == h).astype(F32)
    sel_a = (row == heads + h).astype(F32)
    b_b = jnp.dot(ba, sel_b, precision=HI, preferred_element_type=F32)
    a_b = jnp.dot(ba, sel_a, precision=HI, preferred_element_type=F32)
    beta_s[...] = jax.nn.sigmoid(b_b)
    neg_decay = -jnp.exp(jnp.full((1, LANES), alog_ref[h], F32))
    g_s[...] = neg_decay * jax.nn.softplus(a_b + dtb_ref[h])

    ri = lax.broadcasted_iota(jnp.int32, (CHUNK, CHUNK), 0)
    ci = lax.broadcasted_iota(jnp.int32, (CHUNK, CHUNK), 1)
    tri = ri >= ci
    strict = ri > ci
    ltri = tri.astype(F32)
    utri = (ri <= ci).astype(F32)
    ones = jnp.ones((CHUNK, CHUNK), F32)

    def chunk_a(c, carry):
        r = pl.multiple_of(c * CHUNK, CHUNK)
        q = qs[pl.ds(r, CHUNK), :]
        k = ks[pl.ds(r, CHUNK), :]
        v = vs[pl.ds(r, CHUNK), :]
        beta = beta_s[pl.ds(r, CHUNK), :]
        g = g_s[pl.ds(r, CHUNK), :]
        gcol = jnp.dot(ltri, g, precision=HI, preferred_element_type=F32)
        grow = jnp.dot(ones, utri * g[:, :CHUNK], precision=HI, preferred_element_type=F32)
        decay = jnp.exp(jnp.where(tri, gcol[:, :CHUNK] - grow, -jnp.inf))
        kb = k * beta
        vb = v * beta
        eg = jnp.exp(gcol)
        pw = -jnp.where(strict, _dot_nt(kb, k) * decay, 0.0)
        sol = jnp.concatenate([vb, kb * eg], axis=1)
        n_fac = CHUNK.bit_length() - 1
        for p in range(n_fac):
            sol = sol + _ndot(pw, sol)
            if p + 1 < n_fac:
                pw = _ndot(pw, pw)
        u_s[pl.ds(r, CHUNK), :] = sol[:, :dk]
        r2 = pl.multiple_of(c * (2 * CHUNK), 2 * CHUNK)
        wq_s[pl.ds(r2, CHUNK), :] = sol[:, dk:].astype(BF16)
        wq_s[pl.ds(r2 + CHUNK, CHUNK), :] = (q * eg).astype(BF16)
        gl = gcol[CHUNK - 1:CHUNK, :]
        kt_s[pl.ds(r, CHUNK), :] = (k * jnp.exp(gl - gcol)).astype(BF16)
        attn_s[pl.ds(r, CHUNK), :] = (_dot_nt(q, k) * decay).astype(BF16)
        r8 = pl.multiple_of(c * 8, 8)
        gl_s[pl.ds(r8, 8), :] = jnp.broadcast_to(jnp.exp(gl), (8, dk))
        return carry

    lax.fori_loop(0, nch, chunk_a, 0)

    def chunk_b(c, s):
        r = pl.multiple_of(c * CHUNK, CHUNK)
        r2 = pl.multiple_of(c * (2 * CHUNK), 2 * CHUNK)
        r8 = pl.multiple_of(c * 8, 8)
        t = jnp.dot(wq_s[pl.ds(r2, 2 * CHUNK), :], s.astype(BF16), preferred_element_type=F32)
        v_new = (u_s[pl.ds(r, CHUNK), :] - t[:CHUNK]).astype(BF16)
        o_s[pl.ds(r, CHUNK), :] = t[CHUNK:] + jnp.dot(attn_s[pl.ds(r, CHUNK), :], v_new,
                                                      preferred_element_type=F32)
        return s * gl_s[pl.ds(r8, 1), :] + _dot_tn(kt_s[pl.ds(r, CHUNK), :], v_new)

    state_ref[...] = lax.fori_loop(0, nch, chunk_b, state_ref[...])

    o = _rms(o_s[...], onorm_ref[...]) * _silu(z_ref[0])
    o_ref[0] = o.astype(o_ref.dtype)


def _ndot(a, b):
    return jnp.dot(a, b, precision=HI, preferred_element_type=F32)


def _gdn(p3, ba3, conv_w, a_log, dt_bias, o_norm, heads, ts):
    bsz, t, _ = p3.shape
    dk = LANES
    nch = ts // CHUNK
    col = lambda off: pl.BlockSpec((1, ts, dk), lambda b, h, s: (b, s, off + h))
    cw = lambda off: pl.BlockSpec((conv_w.shape[0], dk), lambda b, h, s: (0, off + h))
    smem = pl.BlockSpec(memory_space=pltpu.SMEM)
    return pl.pallas_call(
        functools.partial(_gdn_kernel, heads=heads, ts=ts),
        grid=(bsz, heads, t // ts),
        in_specs=[smem, smem, col(0), col(heads), col(2 * heads), col(3 * heads),
                  pl.BlockSpec((1, ts, LANES), lambda b, h, s: (b, s, 0)),
                  cw(0), cw(heads), cw(2 * heads),
                  pl.BlockSpec((1, dk), lambda b, h, s: (0, 0))],
        out_specs=pl.BlockSpec((1, ts, dk), lambda b, h, s: (b, s, h)),
        out_shape=jax.ShapeDtypeStruct((bsz, t, heads * dk), BF16),
        scratch_shapes=[
            pltpu.VMEM((dk, dk), F32),
            pltpu.VMEM((ts + 8, dk), F32), pltpu.VMEM((ts + 8, dk), F32), pltpu.VMEM((ts + 8, dk), F32),
            pltpu.VMEM((ts, dk), F32), pltpu.VMEM((ts, dk), F32), pltpu.VMEM((ts, dk), F32),
            pltpu.VMEM((ts, dk), F32), pltpu.VMEM((ts, dk), F32),
            pltpu.VMEM((ts, dk), F32),
            pltpu.VMEM((2 * ts, dk), BF16),
            pltpu.VMEM((ts, dk), BF16),
            pltpu.VMEM((ts, CHUNK), BF16),
            pltpu.VMEM((8 * nch, dk), F32),
            pltpu.VMEM((ts, dk), F32),
        ],
        compiler_params=_cparams(3),
        name="gdn",
    )(a_log, dt_bias, p3, p3, p3, p3, ba3, conv_w, conv_w, conv_w, o_norm.reshape(1, dk))


def _gmlp_kernel(u_ref, v_ref, lng_ref, lnb_ref, ws_ref, bs_ref, o_ref, *, groups, blk):
    tb = u_ref.shape[0]
    pos_i = lax.broadcasted_iota(jnp.int32, (blk, blk), 0) // CHUNK
    pos_j = lax.broadcasted_iota(jnp.int32, (blk, blk), 1) // CHUNK
    mask = pos_j <= pos_i
    for g in range(groups):
        cs = slice(g * LANES, (g + 1) * LANES)
        vg = _gelu(v_ref[:, cs])
        mu = jnp.mean(vg, axis=-1, keepdims=True)
        vc = vg - mu
        vn = vc * lax.rsqrt(jnp.mean(vc * vc, axis=-1, keepdims=True) + EPS)
        vn = (vn * lng_ref[:, cs] + lnb_ref[:, cs]).astype(BF16)
        wm = jnp.where(mask, ws_ref[g], 0.0).astype(BF16)
        for m in range(tb // blk):
            rs = slice(m * blk, (m + 1) * blk)
            mixed = jnp.dot(wm, vn[rs], preferred_element_type=F32) + bs_ref[g]
            u = _gelu(u_ref[rs, cs])
            o_ref[rs, cs] = (u * mixed).astype(o_ref.dtype)


def _gmlp(p2, u_blk, v_blk, ln_g, ln_b, w_s, bs_b, tb):
    m = p2.shape[0]
    groups, blk, _ = w_s.shape
    w = groups * LANES
    return pl.pallas_call(
        functools.partial(_gmlp_kernel, groups=groups, blk=blk),
        grid=(m // tb,),
        in_specs=[
            pl.BlockSpec((tb, w), lambda i: (i, u_blk)),
            pl.BlockSpec((tb, w), lambda i: (i, v_blk)),
            pl.BlockSpec((1, w), lambda i: (0, 0)),
            pl.BlockSpec((1, w), lambda i: (0, 0)),
            pl.BlockSpec((groups, blk, blk), lambda i: (0, 0, 0)),
            pl.BlockSpec((groups, blk, LANES), lambda i: (0, 0, 0)),
        ],
        out_specs=pl.BlockSpec((tb, w), lambda i: (i, 0)),
        out_shape=jax.ShapeDtypeStruct((m, w), BF16),
        compiler_params=_cparams(1),
        name="gmlp",
    )(p2, p2, ln_g.reshape(1, w), ln_b.reshape(1, w), w_s, bs_b)


def _outproj_kernel(x_ref, a_ref, b_ref, wa_ref, wb_ref, o_ref):
    o_ref[...] = (x_ref[...]
                  + jnp.dot(a_ref[...], wa_ref[...], preferred_element_type=F32)
                  + jnp.dot(b_ref[...], wb_ref[...], preferred_element_type=F32))


def _outproj(x2, oa, ob, w_out, bm, bn):
    m, d = x2.shape
    ka = oa.shape[1]
    kb = ob.shape[1]
    return pl.pallas_call(
        _outproj_kernel,
        grid=(m // bm, d // bn),
        in_specs=[
            pl.BlockSpec((bm, bn), lambda i, j: (i, j)),
            pl.BlockSpec((bm, ka), lambda i, j: (i, 0)),
            pl.BlockSpec((bm, kb), lambda i, j: (i, 0)),
            pl.BlockSpec((ka, bn), lambda i, j: (0, j)),
            pl.BlockSpec((kb, bn), lambda i, j: (ka // kb, j)),
        ],
        out_specs=pl.BlockSpec((bm, bn), lambda i, j: (i, j)),
        out_shape=jax.ShapeDtypeStruct((m, d), F32),
        compiler_params=_cparams(2),
        name="outproj",
    )(x2, oa, ob, w_out, w_out)


def _mlp_kernel(x_ref, g_ref, w1_ref, w2_ref, fg_ref, o_ref, xn_ref, *, final_norm):
    k = pl.program_id(1)

    @pl.when(k == 0)
    def _():
        x = x_ref[...]
        xn_ref[...] = _rms(x, g_ref[...]).astype(BF16)
        o_ref[...] = x

    a = jnp.maximum(jnp.dot(xn_ref[...], w1_ref[...], preferred_element_type=F32), 0.0)
    o_ref[...] += jnp.dot((a * a).astype(BF16), w2_ref[...], preferred_element_type=F32)

    if final_norm:
        @pl.when(k == pl.num_programs(1) - 1)
        def _():
            o_ref[...] = _rms(o_ref[...], fg_ref[...])


def _mlp(x2, gain, w1, w2, final_gain, bm, bf, final_norm):
    m, d = x2.shape
    f = w1.shape[1]
    return pl.pallas_call(
        functools.partial(_mlp_kernel, final_norm=final_norm),
        grid=(m // bm, f // bf),
        in_specs=[
            pl.BlockSpec((bm, d), lambda i, k: (i, 0)),
            pl.BlockSpec((1, d), lambda i, k: (0, 0)),
            pl.BlockSpec((d, bf), lambda i, k: (0, k)),
            pl.BlockSpec((bf, d), lambda i, k: (k, 0)),
            pl.BlockSpec((1, d), lambda i, k: (0, 0)),
        ],
        out_specs=pl.BlockSpec((bm, d), lambda i, k: (i, 0)),
        out_shape=jax.ShapeDtypeStruct((m, d), F32),
        scratch_shapes=[pltpu.VMEM((bm, d), BF16)],
        compiler_params=_cparams(2),
        name="mlp_final" if final_norm else "mlp",
    )(x2, gain.reshape(1, d), w1, w2, final_gain.reshape(1, d))


def _glu_kernel(x_ref, g_ref, wa_ref, wb_ref, ba_ref, bb_ref, z_ref, xn_ref):
    @pl.when(pl.program_id(1) == 0)
    def _():
        xn_ref[...] = _rms(x_ref[...], g_ref[...]).astype(BF16)

    xn = xn_ref[...]
    a = jnp.dot(xn, wa_ref[...], preferred_element_type=F32) + ba_ref[...]
    b = jnp.dot(xn, wb_ref[...], preferred_element_type=F32) + bb_ref[...]
    z_ref[...] = a * jax.nn.sigmoid(b)


def _glu(x2, gain, pw1, pw1_b, bm, bn):
    m, d = x2.shape
    nb = d // bn
    return pl.pallas_call(
        _glu_kernel,
        grid=(m // bm, nb),
        in_specs=[
            pl.BlockSpec((bm, d), lambda i, j: (i, 0)),
            pl.BlockSpec((1, d), lambda i, j: (0, 0)),
            pl.BlockSpec((d, bn), lambda i, j: (0, j)),
            pl.BlockSpec((d, bn), lambda i, j: (0, j + nb)),
            pl.BlockSpec((1, bn), lambda i, j: (0, j)),
            pl.BlockSpec((1, bn), lambda i, j: (0, j + nb)),
        ],
        out_specs=pl.BlockSpec((bm, bn), lambda i, j: (i, j)),
        out_shape=jax.ShapeDtypeStruct((m, d), F32),
        scratch_shapes=[pltpu.VMEM((bm, d), BF16)],
        compiler_params=_cparams(2),
        name="glu",
    )(x2, gain.reshape(1, d), pw1, pw1, pw1_b.reshape(1, 2 * d), pw1_b.reshape(1, 2 * d))


def _convpw2_kernel(x_ref, z_ref, zh_ref, dw_ref, dwb_ref, lng_ref, lnb_ref, w_ref, b_ref, o_ref,
                    zp_ref, zs_ref, zc_ref, *, width, tiles_per_seq, rb, cs):
    i = pl.program_id(0)
    bm, d = z_ref.shape

    @pl.when(pl.program_id(1) == 0)
    def _():
        first = (i % tiles_per_seq) == 0
        zp_ref[0:HALO, :] = jnp.where(first, 0.0, zh_ref[...])
        zp_ref[HALO:HALO + bm, :] = z_ref[...]
        base = HALO - (width - 1)

        def rows(rblk, carry):
            r = pl.multiple_of(rblk * rb, rb)
            for c0 in range(0, d, cs):
                out = jnp.broadcast_to(dwb_ref[:, c0:c0 + cs], (rb, cs))
                for s in range(8):
                    part = None
                    for j in range(width):
                        if (base + j) % 8 != s:
                            continue
                        win = zp_ref[pl.ds(r + (base + j - s), rb + (8 if s else 0)), c0:c0 + cs]
                        term = win * dw_ref[j:j + 1, c0:c0 + cs]
                        part = term if part is None else part + term
                    if part is not None:
                        out = out + part[s:s + rb]
                zc_ref[:, c0:c0 + cs] = out
            acc = zc_ref[...]
            mu = jnp.mean(acc, axis=-1, keepdims=True)
            xc = acc - mu
            y = xc * lax.rsqrt(jnp.mean(xc * xc, axis=-1, keepdims=True) + EPS)
            y = y * lng_ref[...] + lnb_ref[...]
            zs_ref[pl.ds(r, rb), :] = _silu(y).astype(BF16)
            return carry

        lax.fori_loop(0, bm // rb, rows, 0)

    o_ref[...] = x_ref[...] + jnp.dot(zs_ref[...], w_ref[...], preferred_element_type=F32) + b_ref[...]


def _convpw2(x2, z2, dw, dw_b, ln_g, ln_b, pw2, pw2_b, seq, bm, bn):
    m, d = x2.shape
    width = dw.shape[0]
    hb = bm // HALO
    row = lambda a: a.reshape(1, d)
    return pl.pallas_call(
        functools.partial(_convpw2_kernel, width=width, tiles_per_seq=seq // bm, rb=16, cs=min(512, d)),
        grid=(m // bm, d // bn),
        in_specs=[
            pl.BlockSpec((bm, bn), lambda i, j: (i, j)),
            pl.BlockSpec((bm, d), lambda i, j: (i, 0)),
            pl.BlockSpec((HALO, d), lambda i, j: (jnp.maximum(i * hb - 1, 0), 0)),
            pl.BlockSpec((width, d), lambda i, j: (0, 0)),
            pl.BlockSpec((1, d), lambda i, j: (0, 0)),
            pl.BlockSpec((1, d), lambda i, j: (0, 0)),
            pl.BlockSpec((1, d), lambda i, j: (0, 0)),
            pl.BlockSpec((d, bn), lambda i, j: (0, j)),
            pl.BlockSpec((1, bn), lambda i, j: (0, j)),
        ],
        out_specs=pl.BlockSpec((bm, bn), lambda i, j: (i, j)),
        out_shape=jax.ShapeDtypeStruct((m, d), F32),
        scratch_shapes=[pltpu.VMEM((bm + HALO, d), F32), pltpu.VMEM((bm, d), BF16),
                        pltpu.VMEM((16, d), F32)],
        compiler_params=_cparams(2),
        name="convpw2",
    )(x2, z2, z2, dw, row(dw_b), row(ln_g), row(ln_b), pw2, row(pw2_b))


def _even_layer(x2, bsz, seq, norm_g, w_in, conv_w, a_log, dt_bias, o_norm, ln_g, ln_b, w_s, b_s, w_out):
    heads = a_log.shape[0]
    groups = w_s.shape[0]
    qkvz = 4 * heads * LANES
    gate0 = qkvz
    uv0 = gate0 + 2 * heads
    w_main = jnp.concatenate([w_in[:, :qkvz], w_in[:, uv0:]], axis=1).astype(BF16)
    w_ba = jnp.pad(w_in[:, gate0:uv0], ((0, 0), (0, LANES - 2 * heads))).astype(BF16)
    p2, ba2 = _inproj(x2, norm_g, w_main, w_ba, bm=min(512, x2.shape[0]), bn=min(1024, w_main.shape[1]))
    p3 = p2.reshape(bsz, seq, -1)
    out_a = _gdn(p3, ba2.reshape(bsz, seq, LANES), conv_w, a_log, dt_bias, o_norm, heads,
                 ts=min(1024, seq))
    gw = groups * LANES
    bs_b = jnp.broadcast_to(b_s[:, :, None], b_s.shape + (LANES,))
    out_b = _gmlp(p2, qkvz // gw, qkvz // gw + 1, ln_g, ln_b, w_s, bs_b, tb=min(256, x2.shape[0]))
    return _outproj(x2, out_a.reshape(bsz * seq, -1), out_b, w_out.astype(BF16),
                    bm=min(512, x2.shape[0]), bn=min(1024, x2.shape[1]))


def _odd_layer(x2, seq, norm_g, pw1, pw1_b, dw, dw_b, ln_g, ln_b, pw2, pw2_b):
    bm = min(512, seq)
    z2 = _glu(x2, norm_g, pw1.astype(BF16), pw1_b, bm=bm, bn=min(1024, x2.shape[1]))
    return _convpw2(x2, z2, dw, dw_b, ln_g, ln_b, pw2.astype(BF16), pw2_b, seq, bm=bm,
                    bn=min(1024, x2.shape[1]))


def kernel(x, e_norm, e_w_in, e_conv_w, e_a_log, e_dt_bias, e_o_norm, e_ln_g, e_ln_b, e_w_s, e_b_s, e_w_out, o_norm, o_pw1, o_pw1_b, o_dw, o_dw_b, o_ln_g, o_ln_b, o_pw2, o_pw2_b, f_norm, f_w1, f_w2, final_norm):
    bsz, seq, d = x.shape
    depth = f_norm.shape[0]
    x2 = x.reshape(bsz * seq, d)
    for layer in range(depth):
        i = layer // 2
        if layer % 2 == 0:
            x2 = _even_layer(x2, bsz, seq, e_norm[i], e_w_in[i], e_conv_w[i], e_a_log[i], e_dt_bias[i],
                             e_o_norm[i], e_ln_g[i], e_ln_b[i], e_w_s[i], e_b_s[i], e_w_out[i])
        else:
            x2 = _odd_layer(x2, seq, o_norm[i], o_pw1[i], o_pw1_b[i], o_dw[i], o_dw_b[i], o_ln_g[i],
                            o_ln_b[i], o_pw2[i], o_pw2_b[i])
        x2 = _mlp(x2, f_norm[layer], f_w1[layer].astype(BF16), f_w2[layer].astype(BF16), final_norm,
                  bm=min(512, x2.shape[0]), bf=min(1024, f_w1.shape[2]),
                  final_norm=(layer == depth - 1))
    return x2.reshape(bsz, seq, d)
```

```python
import functools

import jax
import jax.numpy as jnp
from jax import lax
from jax.experimental import pallas as pl
from jax.experimental.pallas import tpu as pltpu

F32 = jnp.float32
BF16 = jnp.bfloat16
EPS = 1e-6
CHUNK = 64
LANES = 128
HALO = 32
VMEM_LIMIT = 56 * 1024 * 1024
HI = lax.Precision.HIGHEST


def _cparams(n_axes):
    return pltpu.CompilerParams(dimension_semantics=("arbitrary",) * n_axes,
                                vmem_limit_bytes=VMEM_LIMIT)


def _bdot(a, b):
    return jnp.dot(a.astype(BF16), b.astype(BF16), preferred_element_type=F32)


def _dot_nt(a, b):
    return lax.dot_general(a.astype(BF16), b.astype(BF16), (((1,), (1,)), ((), ())),
                           preferred_element_type=F32)


def _dot_tn(a, b):
    return lax.dot_general(a.astype(BF16), b.astype(BF16), (((0,), (0,)), ((), ())),
                           preferred_element_type=F32)


def _rms(x, g):
    return x * lax.rsqrt(jnp.mean(x * x, axis=-1, keepdims=True) + EPS) * g


def _silu(x):
    return x * jax.nn.sigmoid(x)


def _gelu(x):
    return 0.5 * x * (1.0 + lax.erf(x * (2.0 ** -0.5)))


def _inproj_kernel(x_ref, g_ref, w_ref, wba_ref, p_ref, ba_ref, xn_ref):
    @pl.when(pl.program_id(1) == 0)
    def _():
        xn = _rms(x_ref[...], g_ref[...]).astype(BF16)
        xn_ref[...] = xn
        ba_ref[...] = jnp.dot(xn, wba_ref[...], preferred_element_type=F32)

    p_ref[...] = jnp.dot(xn_ref[...], w_ref[...], preferred_element_type=F32)


def _inproj(x2, gain, w_main, w_ba, bm, bn):
    m, d = x2.shape
    n = w_main.shape[1]
    return pl.pallas_call(
        _inproj_kernel,
        grid=(m // bm, n // bn),
        in_specs=[
            pl.BlockSpec((bm, d), lambda i, j: (i, 0)),
            pl.BlockSpec((1, d), lambda i, j: (0, 0)),
            pl.BlockSpec((d, bn), lambda i, j: (0, j)),
            pl.BlockSpec((d, LANES), lambda i, j: (0, 0)),
        ],
        out_specs=[
            pl.BlockSpec((bm, bn), lambda i, j: (i, j)),
            pl.BlockSpec((bm, LANES), lambda i, j: (i, 0)),
        ],
        out_shape=[jax.ShapeDtypeStruct((m, n), F32), jax.ShapeDtypeStruct((m, LANES), F32)],
        scratch_shapes=[pltpu.VMEM((bm, d), BF16)],
        compiler_params=_cparams(2),
        name="inproj",
    )(x2, gain.reshape(1, d), w_main, w_ba)


def _split3(x):
    h1 = x.astype(BF16)
    r1 = x - h1.astype(F32)
    h2 = r1.astype(BF16)
    h3 = (r1 - h2.astype(F32)).astype(BF16)
    return h1, h2, h3


def _exact_dot_l(m01, x):
    m = m01.astype(BF16)
    h1, h2, h3 = _split3(x)
    return (jnp.dot(m, h3, preferred_element_type=F32) + jnp.dot(m, h2, preferred_element_type=F32)
            + jnp.dot(m, h1, preferred_element_type=F32))


def _exact_dot_r(x, m01):
    m = m01.astype(BF16)
    h1, h2, h3 = _split3(x)
    return (jnp.dot(h3, m, preferred_element_type=F32) + jnp.dot(h2, m, preferred_element_type=F32)
            + jnp.dot(h1, m, preferred_element_type=F32))


def _gdn_kernel(q_ref, k_ref, v_ref, z_ref, ba_ref, cwq_ref, cwk_ref, cwv_ref, alog_ref, dtb_ref, onorm_ref,
                o_ref,
                state_ref, qh_ref, kh_ref, vh_ref, beta_s, g_s, gcol_s, grow_s, u_s, wq_s, kt_s, attn_s, gl_s,
                *, heads, hg, ts):
    grp = pl.program_id(1)
    seg = pl.program_id(2)
    nch = ts // CHUNK
    dk = LANES
    w = hg * dk

    @pl.when(seg == 0)
    def _():
        state_ref[...] = jnp.zeros_like(state_ref)
        for halo in (qh_ref, kh_ref, vh_ref):
            halo[...] = jnp.zeros_like(halo)

    ba = ba_ref[0]
    beta_c = jax.nn.sigmoid(ba)
    g_c = -jnp.exp(alog_ref[...]) * jax.nn.softplus(ba + dtb_ref[...])
    sel_row = lax.broadcasted_iota(jnp.int32, (LANES, w), 0)
    sel_head = lax.broadcasted_iota(jnp.int32, (LANES, w), 1) // dk + grp * hg
    beta_s[...] = _exact_dot_r(beta_c, sel_row == sel_head)
    g_s[...] = _exact_dot_r(g_c, sel_row == sel_head + heads)

    ri = lax.broadcasted_iota(jnp.int32, (CHUNK, CHUNK), 0)
    ci = lax.broadcasted_iota(jnp.int32, (CHUNK, CHUNK), 1)
    tri = ri >= ci
    strict = ri > ci
    eye = (ri == ci).astype(F32)
    eye_w = (lax.broadcasted_iota(jnp.int32, (CHUNK, w), 1) % dk
             == lax.broadcasted_iota(jnp.int32, (CHUNK, w), 0))
    ones = jnp.ones((CHUNK, CHUNK), F32)
    for c in range(nch):
        rows = slice(c * CHUNK, (c + 1) * CHUNK)
        gcol = _exact_dot_l(tri, g_s[rows, :])
        gcol_s[rows, :] = gcol
        grow_s[rows, :] = _exact_dot_l(ones, jnp.where(eye_w, gcol, 0.0))

    def conv_tile(raw_ref, halo_ref, cw_ref, c, r, cols):
        back = pl.multiple_of(jnp.maximum(r - 8, 0), 8)
        prev = jnp.where(c == 0, halo_ref[:, cols], raw_ref[0, pl.ds(back, 8), cols])
        win = jnp.concatenate([prev, raw_ref[0, pl.ds(r, CHUNK), cols]], axis=0)
        width = cw_ref.shape[0]
        acc = None
        for j in range(width):
            off = 8 - (width - 1) + j
            term = win[off:off + CHUNK] * cw_ref[j:j + 1, cols]
            acc = term if acc is None else acc + term
        return _silu(acc)

    def l2n(t):
        return t * lax.rsqrt(jnp.sum(t * t, axis=-1, keepdims=True) + EPS)

    def chunk_a(c, carry):
        r = pl.multiple_of(c * CHUNK, CHUNK)
        r2 = pl.multiple_of(c * (2 * CHUNK), 2 * CHUNK)
        r8 = pl.multiple_of(c * 8, 8)
        for hh in range(hg):
            cols = slice(hh * dk, (hh + 1) * dk)
            q = l2n(conv_tile(q_ref, qh_ref, cwq_ref, c, r, cols)) * (dk ** -0.5)
            k = l2n(conv_tile(k_ref, kh_ref, cwk_ref, c, r, cols))
            v = conv_tile(v_ref, vh_ref, cwv_ref, c, r, cols)
            beta = beta_s[pl.ds(r, CHUNK), cols]
            gcol = gcol_s[pl.ds(r, CHUNK), cols]
            grow = grow_s[pl.ds(r, CHUNK), hh * dk:hh * dk + CHUNK]
            decay = jnp.exp(jnp.where(tri, gcol[:, :CHUNK] - grow, -jnp.inf))
            kb = k * beta
            eg = jnp.exp(gcol)
            pw = -jnp.where(strict, _dot_nt(kb, k) * decay, 0.0)
            t_inv = eye + pw
            for _ in range(CHUNK.bit_length() - 2):
                pw = _bdot(pw, pw)
                t_inv = t_inv + _bdot(pw, t_inv)
            sol = _bdot(t_inv, jnp.concatenate([v * beta, kb * eg], axis=1))
            u_s[pl.ds(r, CHUNK), cols] = sol[:, :dk]
            wq_s[pl.ds(r2, CHUNK), cols] = sol[:, dk:].astype(BF16)
            wq_s[pl.ds(r2 + CHUNK, CHUNK), cols] = (q * eg).astype(BF16)
            gl = gcol[CHUNK - 1:CHUNK, :]
            kt_s[pl.ds(r, CHUNK), cols] = (k * jnp.exp(gl - gcol)).astype(BF16)
            attn_s[pl.ds(r, CHUNK), hh * dk:hh * dk + CHUNK] = (_dot_nt(q, k) * decay).astype(BF16)
            gl_s[pl.ds(r8, 8), cols] = jnp.broadcast_to(jnp.exp(gl), (8, dk))
        return carry

    lax.fori_loop(0, nch, chunk_a, 0)
    for halo, raw in ((qh_ref, q_ref), (kh_ref, k_ref), (vh_ref, v_ref)):
        halo[...] = raw[0, ts - 8:ts, :]

    def chunk_b(c, carry):
        r = pl.multiple_of(c * CHUNK, CHUNK)
        r2 = pl.multiple_of(c * (2 * CHUNK), 2 * CHUNK)
        r8 = pl.multiple_of(c * 8, 8)
        for hh in range(hg):
            cols = slice(hh * dk, (hh + 1) * dk)
            s = state_ref[hh]
            t = jnp.dot(wq_s[pl.ds(r2, 2 * CHUNK), cols], s.astype(BF16), preferred_element_type=F32)
            v_new = (u_s[pl.ds(r, CHUNK), cols] - t[:CHUNK]).astype(BF16)
            o = t[CHUNK:] + jnp.dot(attn_s[pl.ds(r, CHUNK), hh * dk:hh * dk + CHUNK], v_new,
                                    preferred_element_type=F32)
            state_ref[hh] = s * gl_s[pl.ds(r8, 1), cols] + _dot_tn(kt_s[pl.ds(r, CHUNK), cols], v_new)
            gated = _rms(o, onorm_ref[...]) * _silu(z_ref[0, pl.ds(r, CHUNK), cols])
            o_ref[0, pl.ds(r, CHUNK), cols] = gated.astype(o_ref.dtype)
        return carry

    lax.fori_loop(0, nch, chunk_b, 0)


def _gdn(p3, ba3, conv_w, a_log, dt_bias, o_norm, heads, hg, ts):
    bsz, t, _ = p3.shape
    dk = LANES
    w = hg * dk
    ngrp = heads // hg
    nch = ts // CHUNK
    col = lambda part: pl.BlockSpec((1, ts, w), lambda b, g, s: (b, s, part * ngrp + g))
    cw = lambda part: pl.BlockSpec((conv_w.shape[0], w), lambda b, g, s: (0, part * ngrp + g))
    vec = pl.BlockSpec((1, LANES), lambda b, g, s: (0, 0))
    gate_lanes = lambda a: jnp.pad(a, (heads, LANES - 2 * heads)).reshape(1, LANES)
    return pl.pallas_call(
        functools.partial(_gdn_kernel, heads=heads, hg=hg, ts=ts),
        grid=(bsz, ngrp, t // ts),
        in_specs=[col(0), col(1), col(2), col(3),
                  pl.BlockSpec((1, ts, LANES), lambda b, g, s: (b, s, 0)),
                  cw(0), cw(1), cw(2), vec, vec, vec],
        out_specs=pl.BlockSpec((1, ts, w), lambda b, g, s: (b, s, g)),
        out_shape=jax.ShapeDtypeStruct((bsz, t, heads * dk), BF16),
        scratch_shapes=[
            pltpu.VMEM((hg, dk, dk), F32),
            pltpu.VMEM((8, w), F32), pltpu.VMEM((8, w), F32), pltpu.VMEM((8, w), F32),
            pltpu.VMEM((ts, w), F32), pltpu.VMEM((ts, w), F32),
            pltpu.VMEM((ts, w), F32), pltpu.VMEM((ts, w), F32),
            pltpu.VMEM((ts, w), F32),
            pltpu.VMEM((2 * ts, w), BF16),
            pltpu.VMEM((ts, w), BF16),
            pltpu.VMEM((ts, w), BF16),
            pltpu.VMEM((8 * nch, w), F32),
        ],
        compiler_params=_cparams(3),
        name="gdn",
    )(p3, p3, p3, p3, ba3, conv_w, conv_w, conv_w, gate_lanes(a_log), gate_lanes(dt_bias),
      o_norm.reshape(1, dk))


def _gmlp_kernel(u_ref, v_ref, lng_ref, lnb_ref, ws_ref, bs_ref, o_ref, *, groups, blk):
    tb = u_ref.shape[0]
    pos_i = lax.broadcasted_iota(jnp.int32, (blk, blk), 0) // CHUNK
    pos_j = lax.broadcasted_iota(jnp.int32, (blk, blk), 1) // CHUNK
    mask = pos_j <= pos_i
    for g in range(groups):
        cs = slice(g * LANES, (g + 1) * LANES)
        vg = _gelu(v_ref[:, cs])
        mu = jnp.mean(vg, axis=-1, keepdims=True)
        vc = vg - mu
        vn = vc * lax.rsqrt(jnp.mean(vc * vc, axis=-1, keepdims=True) + EPS)
        vn = (vn * lng_ref[:, cs] + lnb_ref[:, cs]).astype(BF16)
        wm = jnp.where(mask, ws_ref[g], 0.0).astype(BF16)
        for m in range(tb // blk):
            rs = slice(m * blk, (m + 1) * blk)
            mixed = jnp.dot(wm, vn[rs], preferred_element_type=F32) + bs_ref[g]
            u = _gelu(u_ref[rs, cs])
            o_ref[rs, cs] = (u * mixed).astype(o_ref.dtype)


def _gmlp(p2, u_blk, v_blk, ln_g, ln_b, w_s, bs_b, tb):
    m = p2.shape[0]
    groups, blk, _ = w_s.shape
    w = groups * LANES
    return pl.pallas_call(
        functools.partial(_gmlp_kernel, groups=groups, blk=blk),
        grid=(m // tb,),
        in_specs=[
            pl.BlockSpec((tb, w), lambda i: (i, u_blk)),
            pl.BlockSpec((tb, w), lambda i: (i, v_blk)),
            pl.BlockSpec((1, w), lambda i: (0, 0)),
            pl.BlockSpec((1, w), lambda i: (0, 0)),
            pl.BlockSpec((groups, blk, blk), lambda i: (0, 0, 0)),
            pl.BlockSpec((groups, blk, LANES), lambda i: (0, 0, 0)),
        ],
        out_specs=pl.BlockSpec((tb, w), lambda i: (i, 0)),
        out_shape=jax.ShapeDtypeStruct((m, w), BF16),
        compiler_params=_cparams(1),
        name="gmlp",
    )(p2, p2, ln_g.reshape(1, w), ln_b.reshape(1, w), w_s, bs_b)


def _outproj_kernel(x_ref, a_ref, b_ref, wa_ref, wb_ref, o_ref):
    o_ref[...] = (x_ref[...]
                  + jnp.dot(a_ref[...], wa_ref[...], preferred_element_type=F32)
                  + jnp.dot(b_ref[...], wb_ref[...], preferred_element_type=F32))


def _outproj(x2, oa, ob, w_out, bm, bn):
    m, d = x2.shape
    ka = oa.shape[1]
    kb = ob.shape[1]
    return pl.pallas_call(
        _outproj_kernel,
        grid=(m // bm, d // bn),
        in_specs=[
            pl.BlockSpec((bm, bn), lambda i, j: (i, j)),
            pl.BlockSpec((bm, ka), lambda i, j: (i, 0)),
            pl.BlockSpec((bm, kb), lambda i, j: (i, 0)),
            pl.BlockSpec((ka, bn), lambda i, j: (0, j)),
            pl.BlockSpec((kb, bn), lambda i, j: (ka // kb, j)),
        ],
        out_specs=pl.BlockSpec((bm, bn), lambda i, j: (i, j)),
        out_shape=jax.ShapeDtypeStruct((m, d), F32),
        compiler_params=_cparams(2),
        name="outproj",
    )(x2, oa, ob, w_out, w_out)


def _mlp_kernel(x_ref, g_ref, w1_ref, w2_ref, fg_ref, o_ref, xn_ref, *, final_norm):
    k = pl.program_id(1)

    @pl.when(k == 0)
    def _():
        x = x_ref[...]
        xn_ref[...] = _rms(x, g_ref[...]).astype(BF16)
        o_ref[...] = x

    a = jnp.maximum(jnp.dot(xn_ref[...], w1_ref[...], preferred_element_type=F32), 0.0)
    o_ref[...] += jnp.dot((a * a).astype(BF16), w2_ref[...], preferred_element_type=F32)

    if final_norm:
        @pl.when(k == pl.num_programs(1) - 1)
        def _():
            o_ref[...] = _rms(o_ref[...], fg_ref[...])


def _mlp(x2, gain, w1, w2, final_gain, bm, bf, final_norm):
    m, d = x2.shape
    f = w1.shape[1]
    return pl.pallas_call(
        functools.partial(_mlp_kernel, final_norm=final_norm),
        grid=(m // bm, f // bf),
        in_specs=[
            pl.BlockSpec((bm, d), lambda i, k: (i, 0)),
            pl.BlockSpec((1, d), lambda i, k: (0, 0)),
            pl.BlockSpec((d, bf), lambda i, k: (0, k)),
            pl.BlockSpec((bf, d), lambda i, k: (k, 0)),
            pl.BlockSpec((1, d), lambda i, k: (0, 0)),
        ],
        out_specs=pl.BlockSpec((bm, d), lambda i, k: (i, 0)),
        out_shape=jax.ShapeDtypeStruct((m, d), F32),
        scratch_shapes=[pltpu.VMEM((bm, d), BF16)],
        compiler_params=_cparams(2),
        name="mlp_final" if final_norm else "mlp",
    )(x2, gain.reshape(1, d), w1, w2, final_gain.reshape(1, d))


def _glu_kernel(x_ref, g_ref, wa_ref, wb_ref, ba_ref, bb_ref, z_ref, xn_ref):
    @pl.when(pl.program_id(1) == 0)
    def _():
        xn_ref[...] = _rms(x_ref[...], g_ref[...]).astype(BF16)

    xn = xn_ref[...]
    a = jnp.dot(xn, wa_ref[...], preferred_element_type=F32) + ba_ref[...]
    b = jnp.dot(xn, wb_ref[...], preferred_element_type=F32) + bb_ref[...]
    z_ref[...] = a * jax.nn.sigmoid(b)


def _glu(x2, gain, pw1, pw1_b, bm, bn):
    m, d = x2.shape
    nb = d // bn
    return pl.pallas_call(
        _glu_kernel,
        grid=(m // bm, nb),
        in_specs=[
            pl.BlockSpec((bm, d), lambda i, j: (i, 0)),
            pl.BlockSpec((1, d), lambda i, j: (0, 0)),
            pl.BlockSpec((d, bn), lambda i, j: (0, j)),
            pl.BlockSpec((d, bn), lambda i, j: (0, j + nb)),
            pl.BlockSpec((1, bn), lambda i, j: (0, j)),
            pl.BlockSpec((1, bn), lambda i, j: (0, j + nb)),
        ],
        out_specs=pl.BlockSpec((bm, bn), lambda i, j: (i, j)),
        out_shape=jax.ShapeDtypeStruct((m, d), F32),
        scratch_shapes=[pltpu.VMEM((bm, d), BF16)],
        compiler_params=_cparams(2),
        name="glu",
    )(x2, gain.reshape(1, d), pw1, pw1, pw1_b.reshape(1, 2 * d), pw1_b.reshape(1, 2 * d))


def _convpw2_kernel(x_ref, z_ref, zh_ref, dw_ref, dwb_ref, lng_ref, lnb_ref, w_ref, b_ref, o_ref,
                    zp_ref, zs_ref, zc_ref, *, width, tiles_per_seq, rb):
    i = pl.program_id(0)
    bm, d = z_ref.shape

    @pl.when(pl.program_id(1) == 0)
    def _():
        first = (i % tiles_per_seq) == 0
        zp_ref[0:HALO, :] = jnp.where(first, 0.0, zh_ref[...])
        zp_ref[HALO:HALO + bm, :] = z_ref[...]
        base = HALO - (width - 1)

        def rows(rblk, carry):
            r = pl.multiple_of(rblk * rb, rb)
            for c0 in range(0, d, LANES):
                cols = slice(c0, c0 + LANES)
                out = jnp.broadcast_to(dwb_ref[:, cols], (rb, LANES))
                for s in range(8):
                    part = None
                    for j in range(width):
                        if (base + j) % 8 != s:
                            continue
                        win = zp_ref[pl.ds(r + (base + j - s), rb + (8 if s else 0)), cols]
                        term = win * dw_ref[j:j + 1, cols]
                        part = term if part is None else part + term
                    if part is not None:
                        out = out + part[s:s + rb]
                zc_ref[:, cols] = out
            for r0 in range(0, rb, 16):
                acc = zc_ref[r0:r0 + 16, :]
                mu = jnp.mean(acc, axis=-1, keepdims=True)
                xc = acc - mu
                y = xc * lax.rsqrt(jnp.mean(xc * xc, axis=-1, keepdims=True) + EPS)
                y = y * lng_ref[...] + lnb_ref[...]
                zs_ref[pl.ds(r + r0, 16), :] = _silu(y).astype(BF16)
            return carry

        lax.fori_loop(0, bm // rb, rows, 0)

    o_ref[...] = x_ref[...] + jnp.dot(zs_ref[...], w_ref[...], preferred_element_type=F32) + b_ref[...]


def _convpw2(x2, z2, dw, dw_b, ln_g, ln_b, pw2, pw2_b, seq, bm, bn):
    m, d = x2.shape
    width = dw.shape[0]
    hb = bm // HALO
    row = lambda a: a.reshape(1, d)
    return pl.pallas_call(
        functools.partial(_convpw2_kernel, width=width, tiles_per_seq=seq // bm, rb=CHUNK),
        grid=(m // bm, d // bn),
        in_specs=[
            pl.BlockSpec((bm, bn), lambda i, j: (i, j)),
            pl.BlockSpec((bm, d), lambda i, j: (i, 0)),
            pl.BlockSpec((HALO, d), lambda i, j: (jnp.maximum(i * hb - 1, 0), 0)),
            pl.BlockSpec((width, d), lambda i, j: (0, 0)),
            pl.BlockSpec((1, d), lambda i, j: (0, 0)),
            pl.BlockSpec((1, d), lambda i, j: (0, 0)),
            pl.BlockSpec((1, d), lambda i, j: (0, 0)),
            pl.BlockSpec((d, bn), lambda i, j: (0, j)),
            pl.BlockSpec((1, bn), lambda i, j: (0, j)),
        ],
        out_specs=pl.BlockSpec((bm, bn), lambda i, j: (i, j)),
        out_shape=jax.ShapeDtypeStruct((m, d), F32),
        scratch_shapes=[pltpu.VMEM((bm + HALO, d), F32), pltpu.VMEM((bm, d), BF16),
                        pltpu.VMEM((CHUNK, d), F32)],
        compiler_params=_cparams(2),
        name="convpw2",
    )(x2, z2, z2, dw, row(dw_b), row(ln_g), row(ln_b), pw2, row(pw2_b))


def _even_layer(x2, bsz, seq, norm_g, w_in, conv_w, a_log, dt_bias, o_norm, ln_g, ln_b, w_s, b_s, w_out):
    heads = a_log.shape[0]
    groups = w_s.shape[0]
    qkvz = 4 * heads * LANES
    gate0 = qkvz
    uv0 = gate0 + 2 * heads
    w_main = jnp.concatenate([w_in[:, :qkvz], w_in[:, uv0:]], axis=1).astype(BF16)
    w_ba = jnp.pad(w_in[:, gate0:uv0], ((0, 0), (0, LANES - 2 * heads))).astype(BF16)
    p2, ba2 = _inproj(x2, norm_g, w_main, w_ba, bm=min(512, x2.shape[0]), bn=min(1024, w_main.shape[1]))
    p3 = p2.reshape(bsz, seq, -1)
    out_a = _gdn(p3, ba2.reshape(bsz, seq, LANES), conv_w, a_log, dt_bias, o_norm, heads,
                 hg=min(8, heads), ts=min(512, seq))
    gw = groups * LANES
    bs_b = jnp.broadcast_to(b_s[:, :, None], b_s.shape + (LANES,))
    out_b = _gmlp(p2, qkvz // gw, qkvz // gw + 1, ln_g, ln_b, w_s, bs_b, tb=min(256, x2.shape[0]))
    return _outproj(x2, out_a.reshape(bsz * seq, -1), out_b, w_out.astype(BF16),
                    bm=min(512, x2.shape[0]), bn=min(1024, x2.shape[1]))


def _odd_layer(x2, seq, norm_g, pw1, pw1_b, dw, dw_b, ln_g, ln_b, pw2, pw2_b):
    bm = min(512, seq)
    z2 = _glu(x2, norm_g, pw1.astype(BF16), pw1_b, bm=bm, bn=min(1024, x2.shape[1]))
    return _convpw2(x2, z2, dw, dw_b, ln_g, ln_b, pw2.astype(BF16), pw2_b, seq, bm=bm,
                    bn=min(1024, x2.shape[1]))


def kernel(x, e_norm, e_w_in, e_conv_w, e_a_log, e_dt_bias, e_o_norm, e_ln_g, e_ln_b, e_w_s, e_b_s, e_w_out, o_norm, o_pw1, o_pw1_b, o_dw, o_dw_b, o_ln_g, o_ln_b, o_pw2, o_pw2_b, f_norm, f_w1, f_w2, final_norm):
    bsz, seq, d = x.shape
    depth = f_norm.shape[0]
    x2 = x.reshape(bsz * seq, d)
    for layer in range(depth):
        i = layer // 2
        if layer % 2 == 0:
            x2 = _even_layer(x2, bsz, seq, e_norm[i], e_w_in[i], e_conv_w[i], e_a_log[i], e_dt_bias[i],
                             e_o_norm[i], e_ln_g[i], e_ln_b[i], e_w_s[i], e_b_s[i], e_w_out[i])
        else:
            x2 = _odd_layer(x2, seq, o_norm[i], o_pw1[i], o_pw1_b[i], o_dw[i], o_dw_b[i], o_ln_g[i],
                            o_ln_b[i], o_pw2[i], o_pw2_b[i])
        x2 = _mlp(x2, f_norm[layer], f_w1[layer].astype(BF16), f_w2[layer].astype(BF16), final_norm,
                  bm=min(512, x2.shape[0]), bf=min(1024, f_w1.shape[2]),
                  final_norm=(layer == depth - 1))
    return x2.reshape(bsz, seq, d)
```

```python
import functools

import jax
import jax.numpy as jnp
from jax import lax
from jax.experimental import pallas as pl
from jax.experimental.pallas import tpu as pltpu

F32 = jnp.float32
BF16 = jnp.bfloat16
EPS = 1e-6
CHUNK = 64
LANES = 128
HALO = 32
VMEM_LIMIT = 56 * 1024 * 1024
HI = lax.Precision.HIGHEST


def _cparams(n_axes):
    return pltpu.CompilerParams(dimension_semantics=("arbitrary",) * n_axes,
                                vmem_limit_bytes=VMEM_LIMIT)


def _bdot(a, b):
    return jnp.dot(a.astype(BF16), b.astype(BF16), preferred_element_type=F32)


def _dot_nt(a, b):
    return lax.dot_general(a.astype(BF16), b.astype(BF16), (((1,), (1,)), ((), ())),
                           preferred_element_type=F32)


def _dot_tn(a, b):
    return lax.dot_general(a.astype(BF16), b.astype(BF16), (((0,), (0,)), ((), ())),
                           preferred_element_type=F32)


def _rms(x, g):
    return x * lax.rsqrt(jnp.mean(x * x, axis=-1, keepdims=True) + EPS) * g


def _silu(x):
    return x * jax.nn.sigmoid(x)


def _gelu(x):
    return 0.5 * x * (1.0 + lax.erf(x * (2.0 ** -0.5)))


def _inproj_kernel(x_ref, g_ref, wa_ref, wb_ref, wba_ref, p_ref, ba_ref, xn_ref, *, na):
    j = pl.program_id(1)

    @pl.when(j == 0)
    def _():
        xn = _rms(x_ref[...], g_ref[...]).astype(BF16)
        xn_ref[...] = xn
        ba_ref[...] = jnp.dot(xn, wba_ref[...], preferred_element_type=F32)

    @pl.when(j < na)
    def _():
        p_ref[...] = jnp.dot(xn_ref[...], wa_ref[...], preferred_element_type=F32)

    @pl.when(j >= na)
    def _():
        p_ref[...] = jnp.dot(xn_ref[...], wb_ref[...], preferred_element_type=F32)


def _inproj(x2, gain, w_a, n_a, w_b, w_ba, bm, bn):
    m, d = x2.shape
    na = n_a // bn
    nb = w_b.shape[1] // bn
    n = n_a + w_b.shape[1]
    return pl.pallas_call(
        functools.partial(_inproj_kernel, na=na),
        grid=(m // bm, na + nb),
        in_specs=[
            pl.BlockSpec((bm, d), lambda i, j: (i, 0)),
            pl.BlockSpec((1, d), lambda i, j: (0, 0)),
            pl.BlockSpec((d, bn), lambda i, j: (0, jnp.minimum(j, na - 1))),
            pl.BlockSpec((d, bn), lambda i, j: (0, jnp.maximum(j - na, 0))),
            pl.BlockSpec((d, LANES), lambda i, j: (0, 0)),
        ],
        out_specs=[
            pl.BlockSpec((bm, bn), lambda i, j: (i, j)),
            pl.BlockSpec((bm, LANES), lambda i, j: (i, 0)),
        ],
        out_shape=[jax.ShapeDtypeStruct((m, n), F32), jax.ShapeDtypeStruct((m, LANES), F32)],
        scratch_shapes=[pltpu.VMEM((bm, d), BF16)],
        compiler_params=_cparams(2),
        name="inproj",
    )(x2, gain.reshape(1, d), w_a, w_b, w_ba)


def _split3(x):
    h1 = x.astype(BF16)
    r1 = x - h1.astype(F32)
    h2 = r1.astype(BF16)
    h3 = (r1 - h2.astype(F32)).astype(BF16)
    return h1, h2, h3


def _exact_dot_l(m01, x):
    m = m01.astype(BF16)
    h1, h2, h3 = _split3(x)
    return (jnp.dot(m, h3, preferred_element_type=F32) + jnp.dot(m, h2, preferred_element_type=F32)
            + jnp.dot(m, h1, preferred_element_type=F32))


def _exact_dot_r(x, m01):
    m = m01.astype(BF16)
    h1, h2, h3 = _split3(x)
    return (jnp.dot(h3, m, preferred_element_type=F32) + jnp.dot(h2, m, preferred_element_type=F32)
            + jnp.dot(h1, m, preferred_element_type=F32))


def _gdn_kernel(q_ref, k_ref, v_ref, z_ref, ba_ref, cwq_ref, cwk_ref, cwv_ref, alog_ref, dtb_ref, onorm_ref,
                o_ref,
                state_ref, stateb_ref, qh_ref, kh_ref, vh_ref, beta_s, gcol_s, grow_s, npw_s, vb_s, kbe_s,
                u_s, wq_s, kt_s, attn_s, gl_s, *, heads, hg, ts):
    grp = pl.program_id(1)
    seg = pl.program_id(2)
    nch = ts // CHUNK
    dk = LANES
    w = hg * dk

    @pl.when(seg == 0)
    def _():
        state_ref[...] = jnp.zeros_like(state_ref)
        stateb_ref[...] = jnp.zeros_like(stateb_ref)
        for halo in (qh_ref, kh_ref, vh_ref):
            halo[...] = jnp.zeros_like(halo)

    ba = ba_ref[0]
    beta_c = jax.nn.sigmoid(ba)
    g_c = -jnp.exp(alog_ref[...]) * jax.nn.softplus(ba + dtb_ref[...])
    sel_row = lax.broadcasted_iota(jnp.int32, (LANES, w), 0)
    sel_head = lax.broadcasted_iota(jnp.int32, (LANES, w), 1) // dk + grp * hg
    beta_s[...] = _exact_dot_r(beta_c, sel_row == sel_head)

    ri = lax.broadcasted_iota(jnp.int32, (CHUNK, CHUNK), 0)
    ci = lax.broadcasted_iota(jnp.int32, (CHUNK, CHUNK), 1)
    tri = ri >= ci
    strict = ri > ci
    eye = (ri == ci).astype(F32)
    eye_w = (lax.broadcasted_iota(jnp.int32, (CHUNK, w), 1) % dk
             == lax.broadcasted_iota(jnp.int32, (CHUNK, w), 0))
    ones = jnp.ones((CHUNK, CHUNK), F32)
    gc_c = jnp.concatenate([_exact_dot_l(tri, g_c[c * CHUNK:(c + 1) * CHUNK]) for c in range(nch)], axis=0)
    gcol_s[...] = _exact_dot_r(gc_c, sel_row == sel_head + heads)
    for c in range(nch):
        rows = slice(c * CHUNK, (c + 1) * CHUNK)
        grow_s[rows, :] = _exact_dot_l(ones, jnp.where(eye_w, gcol_s[rows, :], 0.0))

    def conv_window(raw_ref, halo_ref, c, r, cols):
        back = pl.multiple_of(jnp.maximum(r - 8, 0), 8)
        prev = jnp.where(c == 0, halo_ref[:, cols], raw_ref[0, pl.ds(back, 8), cols])
        return jnp.concatenate([prev, raw_ref[0, pl.ds(r, CHUNK), cols]], axis=0)

    def conv_silu(win, cw_ref, cols):
        width = cw_ref.shape[0]
        acc = None
        for j in range(width):
            back = width - 1 - j
            shifted = pltpu.roll(win, back, axis=0) if back else win
            term = shifted[8:] * cw_ref[j:j + 1, cols]
            acc = term if acc is None else acc + term
        return _silu(acc)

    def l2n(t):
        return t * lax.rsqrt(jnp.sum(t * t, axis=-1, keepdims=True) + EPS)

    head_cols = [slice(hh * dk, (hh + 1) * dk) for hh in range(hg)]
    half_cols = [slice(hh * dk, hh * dk + CHUNK) for hh in range(hg)]

    def rows_of(c):
        return pl.ds(pl.multiple_of(c * CHUNK, CHUNK), CHUNK)

    def prepare(c, heads_todo):
        r = pl.multiple_of(c * CHUNK, CHUNK)
        r2 = pl.multiple_of(c * (2 * CHUNK), 2 * CHUNK)
        r8 = pl.multiple_of(c * 8, 8)
        for hh in heads_todo:
            cols, half = head_cols[hh], half_cols[hh]
            q = l2n(conv_silu(conv_window(q_ref, qh_ref, c, r, cols), cwq_ref, cols)) * (dk ** -0.5)
            k = l2n(conv_silu(conv_window(k_ref, kh_ref, c, r, cols), cwk_ref, cols))
            v = conv_silu(conv_window(v_ref, vh_ref, c, r, cols), cwv_ref, cols)
            beta = beta_s[pl.ds(r, CHUNK), cols]
            gcol = gcol_s[pl.ds(r, CHUNK), cols]
            decay = jnp.exp(jnp.where(tri, gcol[:, :CHUNK] - grow_s[pl.ds(r, CHUNK), half], -jnp.inf))
            kb = k * beta
            eg = jnp.exp(gcol)
            gl = gcol[CHUNK - 1:CHUNK, :]
            npw_s[pl.ds(r, CHUNK), half] = -jnp.where(strict, _dot_nt(kb, k) * decay, 0.0)
            vb_s[pl.ds(r, CHUNK), cols] = (v * beta).astype(BF16)
            kbe_s[pl.ds(r, CHUNK), cols] = (kb * eg).astype(BF16)
            wq_s[pl.ds(r2 + CHUNK, CHUNK), cols] = (q * eg).astype(BF16)
            kt_s[pl.ds(r, CHUNK), cols] = (k * jnp.exp(gl - gcol)).astype(BF16)
            attn_s[pl.ds(r, CHUNK), half] = (_dot_nt(q, k) * decay).astype(BF16)
            gl_s[pl.ds(r8, 8), cols] = jnp.broadcast_to(jnp.exp(gl), (8, dk))

    def load_npw(c):
        return [npw_s[rows_of(c), half] for half in half_cols]

    def solve_stages(c, npw):
        r = pl.multiple_of(c * CHUNK, CHUNK)
        r2 = pl.multiple_of(c * (2 * CHUNK), 2 * CHUNK)
        t_inv = [eye + p for p in npw]
        pw = [_bdot(p, p) for p in npw]
        yield
        for _ in range(CHUNK.bit_length() - 3):
            pw_b = [p.astype(BF16) for p in pw]
            t_inv = [t + jnp.dot(p, t.astype(BF16), preferred_element_type=F32) for p, t in zip(pw_b, t_inv)]
            pw = [jnp.dot(p, p, preferred_element_type=F32) for p in pw_b]
            yield
        t_inv = [t + _bdot(p, t) for p, t in zip(pw, t_inv)]
        yield
        for cols, t in zip(head_cols, t_inv):
            rhs = jnp.concatenate([vb_s[pl.ds(r, CHUNK), cols], kbe_s[pl.ds(r, CHUNK), cols]], axis=1)
            sol = jnp.dot(t.astype(BF16), rhs, preferred_element_type=F32)
            u_s[pl.ds(r, CHUNK), cols] = sol[:, :dk]
            wq_s[pl.ds(r2, CHUNK), cols] = sol[:, dk:].astype(BF16)

    prepare(0, range(hg))

    def chunk_a(c, carry):
        stages = solve_stages(c - 1, load_npw(c - 1))
        for hh in range(hg):
            next(stages, None)
            prepare(c, [hh])
        for _ in stages:
            pass
        return carry

    lax.fori_loop(1, nch, chunk_a, 0)
    for _ in solve_stages(nch - 1, load_npw(nch - 1)):
        pass
    for halo, raw in ((qh_ref, q_ref), (kh_ref, k_ref), (vh_ref, v_ref)):
        halo[...] = raw[0, ts - 8:ts, :]

    def chunk_b(c, carry):
        r = pl.multiple_of(c * CHUNK, CHUNK)
        r2 = pl.multiple_of(c * (2 * CHUNK), 2 * CHUNK)
        r8 = pl.multiple_of(c * 8, 8)
        hs = range(hg)
        t = [jnp.dot(wq_s[pl.ds(r2, 2 * CHUNK), head_cols[h]], stateb_ref[h], preferred_element_type=F32)
             for h in hs]
        v_new = [(u_s[pl.ds(r, CHUNK), head_cols[h]] - t[h][:CHUNK]).astype(BF16) for h in hs]
        o = [t[h][CHUNK:] + jnp.dot(attn_s[pl.ds(r, CHUNK), half_cols[h]], v_new[h], preferred_element_type=F32)
             for h in hs]
        upd = [_dot_tn(kt_s[pl.ds(r, CHUNK), head_cols[h]], v_new[h]) for h in hs]
        for h in hs:
            s_new = state_ref[h] * gl_s[pl.ds(r8, 1), head_cols[h]] + upd[h]
            state_ref[h] = s_new
            stateb_ref[h] = s_new.astype(BF16)
            gated = _rms(o[h], onorm_ref[...]) * _silu(z_ref[0, pl.ds(r, CHUNK), head_cols[h]])
            o_ref[0, pl.ds(r, CHUNK), head_cols[h]] = gated.astype(o_ref.dtype)
        return carry

    lax.fori_loop(0, nch, chunk_b, 0)


def _gdn(p3, ba3, conv_w, a_log, dt_bias, o_norm, heads, hg, ts):
    bsz, t, _ = p3.shape
    dk = LANES
    w = hg * dk
    ngrp = heads // hg
    nch = ts // CHUNK
    col = lambda part: pl.BlockSpec((1, ts, w), lambda b, g, s: (b, s, part * ngrp + g))
    cw = lambda part: pl.BlockSpec((conv_w.shape[0], w), lambda b, g, s: (0, part * ngrp + g))
    vec = pl.BlockSpec((1, LANES), lambda b, g, s: (0, 0))
    gate_lanes = lambda a: jnp.pad(a, (heads, LANES - 2 * heads)).reshape(1, LANES)
    return pl.pallas_call(
        functools.partial(_gdn_kernel, heads=heads, hg=hg, ts=ts),
        grid=(bsz, ngrp, t // ts),
        in_specs=[col(0), col(1), col(2), col(3),
                  pl.BlockSpec((1, ts, LANES), lambda b, g, s: (b, s, 0)),
                  cw(0), cw(1), cw(2), vec, vec, vec],
        out_specs=pl.BlockSpec((1, ts, w), lambda b, g, s: (b, s, g)),
        out_shape=jax.ShapeDtypeStruct((bsz, t, heads * dk), BF16),
        scratch_shapes=[
            pltpu.VMEM((hg, dk, dk), F32), pltpu.VMEM((hg, dk, dk), BF16),
            pltpu.VMEM((8, w), F32), pltpu.VMEM((8, w), F32), pltpu.VMEM((8, w), F32),
            pltpu.VMEM((ts, w), F32),
            pltpu.VMEM((ts, w), F32), pltpu.VMEM((ts, w), F32),
            pltpu.VMEM((ts, w), F32),
            pltpu.VMEM((ts, w), BF16), pltpu.VMEM((ts, w), BF16),
            pltpu.VMEM((ts, w), F32),
            pltpu.VMEM((2 * ts, w), BF16),
            pltpu.VMEM((ts, w), BF16),
            pltpu.VMEM((ts, w), BF16),
            pltpu.VMEM((8 * nch, w), F32),
        ],
        compiler_params=_cparams(3),
        name="gdn",
    )(p3, p3, p3, p3, ba3, conv_w, conv_w, conv_w, gate_lanes(a_log), gate_lanes(dt_bias),
      o_norm.reshape(1, dk))


def _gmlp_kernel(u_ref, v_ref, lng_ref, lnb_ref, ws_ref, bs_ref, o_ref, *, groups, blk):
    tb = u_ref.shape[0]
    pos_i = lax.broadcasted_iota(jnp.int32, (blk, blk), 0) // CHUNK
    pos_j = lax.broadcasted_iota(jnp.int32, (blk, blk), 1) // CHUNK
    mask = pos_j <= pos_i
    for g in range(groups):
        cs = slice(g * LANES, (g + 1) * LANES)
        vg = _gelu(v_ref[:, cs])
        mu = jnp.mean(vg, axis=-1, keepdims=True)
        vc = vg - mu
        vn = vc * lax.rsqrt(jnp.mean(vc * vc, axis=-1, keepdims=True) + EPS)
        vn = (vn * lng_ref[:, cs] + lnb_ref[:, cs]).astype(BF16)
        wm = jnp.where(mask, ws_ref[g], 0.0).astype(BF16)
        for m in range(tb // blk):
            rs = slice(m * blk, (m + 1) * blk)
            mixed = jnp.dot(wm, vn[rs], preferred_element_type=F32) + bs_ref[g]
            u = _gelu(u_ref[rs, cs])
            o_ref[rs, cs] = (u * mixed).astype(o_ref.dtype)


def _gmlp(p2, u_blk, v_blk, ln_g, ln_b, w_s, bs_b, tb):
    m = p2.shape[0]
    groups, blk, _ = w_s.shape
    w = groups * LANES
    return pl.pallas_call(
        functools.partial(_gmlp_kernel, groups=groups, blk=blk),
        grid=(m // tb,),
        in_specs=[
            pl.BlockSpec((tb, w), lambda i: (i, u_blk)),
            pl.BlockSpec((tb, w), lambda i: (i, v_blk)),
            pl.BlockSpec((1, w), lambda i: (0, 0)),
            pl.BlockSpec((1, w), lambda i: (0, 0)),
            pl.BlockSpec((groups, blk, blk), lambda i: (0, 0, 0)),
            pl.BlockSpec((groups, blk, LANES), lambda i: (0, 0, 0)),
        ],
        out_specs=pl.BlockSpec((tb, w), lambda i: (i, 0)),
        out_shape=jax.ShapeDtypeStruct((m, w), BF16),
        compiler_params=_cparams(1),
        name="gmlp",
    )(p2, p2, ln_g.reshape(1, w), ln_b.reshape(1, w), w_s, bs_b)


def _outproj_kernel(x_ref, a_ref, b_ref, wa_ref, wb_ref, o_ref):
    o_ref[...] = (x_ref[...]
                  + jnp.dot(a_ref[...], wa_ref[...], preferred_element_type=F32)
                  + jnp.dot(b_ref[...], wb_ref[...], preferred_element_type=F32))


def _outproj(x2, oa, ob, w_out, bm, bn):
    m, d = x2.shape
    ka = oa.shape[1]
    kb = ob.shape[1]
    return pl.pallas_call(
        _outproj_kernel,
        grid=(m // bm, d // bn),
        in_specs=[
            pl.BlockSpec((bm, bn), lambda i, j: (i, j)),
            pl.BlockSpec((bm, ka), lambda i, j: (i, 0)),
            pl.BlockSpec((bm, kb), lambda i, j: (i, 0)),
            pl.BlockSpec((ka, bn), lambda i, j: (0, j)),
            pl.BlockSpec((kb, bn), lambda i, j: (ka // kb, j)),
        ],
        out_specs=pl.BlockSpec((bm, bn), lambda i, j: (i, j)),
        out_shape=jax.ShapeDtypeStruct((m, d), F32),
        compiler_params=_cparams(2),
        name="outproj",
    )(x2, oa, ob, w_out, w_out)


def _mlp_kernel(x_ref, g_ref, w1_ref, w2_ref, fg_ref, o_ref, xn_ref, *, final_norm):
    k = pl.program_id(1)

    @pl.when(k == 0)
    def _():
        x = x_ref[...]
        xn_ref[...] = _rms(x, g_ref[...]).astype(BF16)
        o_ref[...] = x

    a = jnp.maximum(jnp.dot(xn_ref[...], w1_ref[...], preferred_element_type=F32), 0.0)
    o_ref[...] += jnp.dot((a * a).astype(BF16), w2_ref[...], preferred_element_type=F32)

    if final_norm:
        @pl.when(k == pl.num_programs(1) - 1)
        def _():
            o_ref[...] = _rms(o_ref[...], fg_ref[...])


def _mlp(x2, gain, w1, w2, final_gain, bm, bf, final_norm):
    m, d = x2.shape
    f = w1.shape[1]
    return pl.pallas_call(
        functools.partial(_mlp_kernel, final_norm=final_norm),
        grid=(m // bm, f // bf),
        in_specs=[
            pl.BlockSpec((bm, d), lambda i, k: (i, 0)),
            pl.BlockSpec((1, d), lambda i, k: (0, 0)),
            pl.BlockSpec((d, bf), lambda i, k: (0, k)),
            pl.BlockSpec((bf, d), lambda i, k: (k, 0)),
            pl.BlockSpec((1, d), lambda i, k: (0, 0)),
        ],
        out_specs=pl.BlockSpec((bm, d), lambda i, k: (i, 0)),
        out_shape=jax.ShapeDtypeStruct((m, d), F32),
        scratch_shapes=[pltpu.VMEM((bm, d), BF16)],
        compiler_params=_cparams(2),
        name="mlp_final" if final_norm else "mlp",
    )(x2, gain.reshape(1, d), w1, w2, final_gain.reshape(1, d))


def _glu_kernel(x_ref, g_ref, wa_ref, wb_ref, ba_ref, bb_ref, z_ref, xn_ref):
    @pl.when(pl.program_id(1) == 0)
    def _():
        xn_ref[...] = _rms(x_ref[...], g_ref[...]).astype(BF16)

    xn = xn_ref[...]
    a = jnp.dot(xn, wa_ref[...], preferred_element_type=F32) + ba_ref[...]
    b = jnp.dot(xn, wb_ref[...], preferred_element_type=F32) + bb_ref[...]
    z_ref[...] = a * jax.nn.sigmoid(b)


def _glu(x2, gain, pw1, pw1_b, bm, bn):
    m, d = x2.shape
    nb = d // bn
    return pl.pallas_call(
        _glu_kernel,
        grid=(m // bm, nb),
        in_specs=[
            pl.BlockSpec((bm, d), lambda i, j: (i, 0)),
            pl.BlockSpec((1, d), lambda i, j: (0, 0)),
            pl.BlockSpec((d, bn), lambda i, j: (0, j)),
            pl.BlockSpec((d, bn), lambda i, j: (0, j + nb)),
            pl.BlockSpec((1, bn), lambda i, j: (0, j)),
            pl.BlockSpec((1, bn), lambda i, j: (0, j + nb)),
        ],
        out_specs=pl.BlockSpec((bm, bn), lambda i, j: (i, j)),
        out_shape=jax.ShapeDtypeStruct((m, d), F32),
        scratch_shapes=[pltpu.VMEM((bm, d), BF16)],
        compiler_params=_cparams(2),
        name="glu",
    )(x2, gain.reshape(1, d), pw1, pw1, pw1_b.reshape(1, 2 * d), pw1_b.reshape(1, 2 * d))


def _convpw2_kernel(x_ref, z_ref, zh_ref, dw_ref, dwb_ref, lng_ref, lnb_ref, w_ref, b_ref, o_ref,
                    zp_ref, zs_ref, zc_ref, *, width, tiles_per_seq, rb):
    i = pl.program_id(0)
    bm, d = z_ref.shape

    @pl.when(pl.program_id(1) == 0)
    def _():
        first = (i % tiles_per_seq) == 0
        zp_ref[0:HALO, :] = jnp.where(first, 0.0, zh_ref[...])
        zp_ref[HALO:HALO + bm, :] = z_ref[...]
        base = HALO - (width - 1)

        def rows(rblk, carry):
            r = pl.multiple_of(rblk * rb, rb)
            for c0 in range(0, d, LANES):
                cols = slice(c0, c0 + LANES)
                out = jnp.broadcast_to(dwb_ref[:, cols], (rb, LANES))
                for s in range(8):
                    part = None
                    for j in range(width):
                        if (base + j) % 8 != s:
                            continue
                        win = zp_ref[pl.ds(r + (base + j - s), rb + (8 if s else 0)), cols]
                        term = win * dw_ref[j:j + 1, cols]
                        part = term if part is None else part + term
                    if part is not None:
                        out = out + part[s:s + rb]
                zc_ref[:, cols] = out
            for r0 in range(0, rb, 16):
                acc = zc_ref[r0:r0 + 16, :]
                mu = jnp.mean(acc, axis=-1, keepdims=True)
                xc = acc - mu
                y = xc * lax.rsqrt(jnp.mean(xc * xc, axis=-1, keepdims=True) + EPS)
                y = y * lng_ref[...] + lnb_ref[...]
                zs_ref[pl.ds(r + r0, 16), :] = _silu(y).astype(BF16)
            return carry

        lax.fori_loop(0, bm // rb, rows, 0)

    o_ref[...] = x_ref[...] + jnp.dot(zs_ref[...], w_ref[...], preferred_element_type=F32) + b_ref[...]


def _convpw2(x2, z2, dw, dw_b, ln_g, ln_b, pw2, pw2_b, seq, bm, bn):
    m, d = x2.shape
    width = dw.shape[0]
    hb = bm // HALO
    row = lambda a: a.reshape(1, d)
    return pl.pallas_call(
        functools.partial(_convpw2_kernel, width=width, tiles_per_seq=seq // bm, rb=CHUNK),
        grid=(m // bm, d // bn),
        in_specs=[
            pl.BlockSpec((bm, bn), lambda i, j: (i, j)),
            pl.BlockSpec((bm, d), lambda i, j: (i, 0)),
            pl.BlockSpec((HALO, d), lambda i, j: (jnp.maximum(i * hb - 1, 0), 0)),
            pl.BlockSpec((width, d), lambda i, j: (0, 0)),
            pl.BlockSpec((1, d), lambda i, j: (0, 0)),
            pl.BlockSpec((1, d), lambda i, j: (0, 0)),
            pl.BlockSpec((1, d), lambda i, j: (0, 0)),
            pl.BlockSpec((d, bn), lambda i, j: (0, j)),
            pl.BlockSpec((1, bn), lambda i, j: (0, j)),
        ],
        out_specs=pl.BlockSpec((bm, bn), lambda i, j: (i, j)),
        out_shape=jax.ShapeDtypeStruct((m, d), F32),
        scratch_shapes=[pltpu.VMEM((bm + HALO, d), F32), pltpu.VMEM((bm, d), BF16),
                        pltpu.VMEM((CHUNK, d), F32)],
        compiler_params=_cparams(2),
        name="convpw2",
    )(x2, z2, z2, dw, row(dw_b), row(ln_g), row(ln_b), pw2, row(pw2_b))


def _tiles(rows, seq, d, d_ff, in_cols, heads):
    return dict(
        inproj=dict(bm=min(1024, rows), bn=min(1024, in_cols)),
        gdn=dict(hg=min(8, heads), ts=min(512, seq)),
        gmlp=dict(tb=min(256, rows)),
        outproj=dict(bm=min(512, rows), bn=d),
        mlp=dict(bm=min(512, rows), bf=min(1024, d_ff)),
        glu=dict(bm=min(1024, rows), bn=min(1024, d)),
        convpw2=dict(bm=min(512, seq), bn=d),
    )


def _even_layer(x2, bsz, seq, tiles, norm_g, w_in, conv_w, a_log, dt_bias, o_norm, ln_g, ln_b, w_s, b_s, w_out):
    heads = a_log.shape[0]
    groups = w_s.shape[0]
    qkvz = 4 * heads * LANES
    gate0 = qkvz
    uv0 = gate0 + 2 * heads
    w_bf = w_in.astype(BF16)
    w_ba = jnp.pad(w_bf[:, gate0:uv0], ((0, 0), (0, LANES - 2 * heads)))
    p2, ba2 = _inproj(x2, norm_g, w_bf, qkvz, w_bf[:, uv0:], w_ba, **tiles["inproj"])
    p3 = p2.reshape(bsz, seq, -1)
    out_a = _gdn(p3, ba2.reshape(bsz, seq, LANES), conv_w, a_log, dt_bias, o_norm, heads, **tiles["gdn"])
    gw = groups * LANES
    bs_b = jnp.broadcast_to(b_s[:, :, None], b_s.shape + (LANES,))
    out_b = _gmlp(p2, qkvz // gw, qkvz // gw + 1, ln_g, ln_b, w_s, bs_b, **tiles["gmlp"])
    return _outproj(x2, out_a.reshape(bsz * seq, -1), out_b, w_out.astype(BF16), **tiles["outproj"])


def _odd_layer(x2, seq, tiles, norm_g, pw1, pw1_b, dw, dw_b, ln_g, ln_b, pw2, pw2_b):
    z2 = _glu(x2, norm_g, pw1.astype(BF16), pw1_b, **tiles["glu"])
    return _convpw2(x2, z2, dw, dw_b, ln_g, ln_b, pw2.astype(BF16), pw2_b, seq, **tiles["convpw2"])


def kernel(x, e_norm, e_w_in, e_conv_w, e_a_log, e_dt_bias, e_o_norm, e_ln_g, e_ln_b, e_w_s, e_b_s, e_w_out, o_norm, o_pw1, o_pw1_b, o_dw, o_dw_b, o_ln_g, o_ln_b, o_pw2, o_pw2_b, f_norm, f_w1, f_w2, final_norm):
    bsz, seq, d = x.shape
    depth = f_norm.shape[0]
    x2 = x.reshape(bsz * seq, d)
    heads = e_a_log.shape[1]
    tiles = _tiles(bsz * seq, seq, d, f_w1.shape[2], e_w_in.shape[2] - 2 * heads, heads)
    for layer in range(depth):
        i = layer // 2
        if layer % 2 == 0:
            x2 = _even_layer(x2, bsz, seq, tiles, e_norm[i], e_w_in[i], e_conv_w[i], e_a_log[i], e_dt_bias[i],
                             e_o_norm[i], e_ln_g[i], e_ln_b[i], e_w_s[i], e_b_s[i], e_w_out[i])
        else:
            x2 = _odd_layer(x2, seq, tiles, o_norm[i], o_pw1[i], o_pw1_b[i], o_dw[i], o_dw_b[i], o_ln_g[i],
                            o_ln_b[i], o_pw2[i], o_pw2_b[i])
        x2 = _mlp(x2, f_norm[layer], f_w1[layer].astype(BF16), f_w2[layer].astype(BF16), final_norm,
                  final_norm=(layer == depth - 1), **tiles["mlp"])
    return x2.reshape(bsz, seq, d)
```

```python
import functools

import jax
import jax.numpy as jnp
from jax import lax
from jax.experimental import pallas as pl
from jax.experimental.pallas import tpu as pltpu

F32 = jnp.float32
BF16 = jnp.bfloat16
EPS = 1e-6
CHUNK = 64
LANES = 128
HALO = 32
VMEM_LIMIT = 56 * 1024 * 1024
HI = lax.Precision.HIGHEST


def _cparams(n_axes):
    return pltpu.CompilerParams(dimension_semantics=("arbitrary",) * n_axes,
                                vmem_limit_bytes=VMEM_LIMIT)


def _bdot(a, b):
    return jnp.dot(a.astype(BF16), b.astype(BF16), preferred_element_type=F32)


def _dot_nt(a, b):
    return lax.dot_general(a.astype(BF16), b.astype(BF16), (((1,), (1,)), ((), ())),
                           preferred_element_type=F32)


def _dot_tn(a, b):
    return lax.dot_general(a.astype(BF16), b.astype(BF16), (((0,), (0,)), ((), ())),
                           preferred_element_type=F32)


def _rms(x, g):
    return x * lax.rsqrt(jnp.mean(x * x, axis=-1, keepdims=True) + EPS) * g


def _silu(x):
    return x * jax.nn.sigmoid(x)


def _gelu(x):
    return 0.5 * x * (1.0 + lax.erf(x * (2.0 ** -0.5)))


def _inproj_kernel(x_ref, g_ref, wa_ref, wb_ref, wba_ref, p_ref, ba_ref, xn_ref, *, na):
    j = pl.program_id(1)

    @pl.when(j == 0)
    def _():
        xn = _rms(x_ref[...], g_ref[...]).astype(BF16)
        xn_ref[...] = xn
        ba_ref[...] = jnp.dot(xn, wba_ref[...], preferred_element_type=F32)

    @pl.when(j < na)
    def _():
        p_ref[...] = jnp.dot(xn_ref[...], wa_ref[...], preferred_element_type=F32)

    @pl.when(j >= na)
    def _():
        p_ref[...] = jnp.dot(xn_ref[...], wb_ref[...], preferred_element_type=F32)


def _inproj(x2, gain, w_a, n_a, w_b, w_ba, bm, bn):
    m, d = x2.shape
    na = n_a // bn
    nb = w_b.shape[1] // bn
    n = n_a + w_b.shape[1]
    return pl.pallas_call(
        functools.partial(_inproj_kernel, na=na),
        grid=(m // bm, na + nb),
        in_specs=[
            pl.BlockSpec((bm, d), lambda i, j: (i, 0)),
            pl.BlockSpec((1, d), lambda i, j: (0, 0)),
            pl.BlockSpec((d, bn), lambda i, j: (0, jnp.minimum(j, na - 1))),
            pl.BlockSpec((d, bn), lambda i, j: (0, jnp.maximum(j - na, 0))),
            pl.BlockSpec((d, LANES), lambda i, j: (0, 0)),
        ],
        out_specs=[
            pl.BlockSpec((bm, bn), lambda i, j: (i, j)),
            pl.BlockSpec((bm, LANES), lambda i, j: (i, 0)),
        ],
        out_shape=[jax.ShapeDtypeStruct((m, n), F32), jax.ShapeDtypeStruct((m, LANES), F32)],
        scratch_shapes=[pltpu.VMEM((bm, d), BF16)],
        compiler_params=_cparams(2),
        name="inproj",
    )(x2, gain.reshape(1, d), w_a, w_b, w_ba)


def _split3(x):
    h1 = x.astype(BF16)
    r1 = x - h1.astype(F32)
    h2 = r1.astype(BF16)
    h3 = (r1 - h2.astype(F32)).astype(BF16)
    return h1, h2, h3


def _exact_dot_l(m01, x):
    m = m01.astype(BF16)
    h1, h2, h3 = _split3(x)
    return (jnp.dot(m, h3, preferred_element_type=F32) + jnp.dot(m, h2, preferred_element_type=F32)
            + jnp.dot(m, h1, preferred_element_type=F32))


def _exact_dot_r(x, m01):
    m = m01.astype(BF16)
    h1, h2, h3 = _split3(x)
    return (jnp.dot(h3, m, preferred_element_type=F32) + jnp.dot(h2, m, preferred_element_type=F32)
            + jnp.dot(h1, m, preferred_element_type=F32))


def _gdn_kernel(q_ref, k_ref, v_ref, z_ref, ba_ref, cwq_ref, cwk_ref, cwv_ref, alog_ref, dtb_ref, onorm_ref,
                o_ref,
                state_ref, stateb_ref, qh_ref, kh_ref, vh_ref, beta_s, gcol_s, grow_s, npw_s, vb_s, kbe_s,
                u_s, wq_s, kt_s, attn_s, gl_s, *, heads, hg, ts):
    grp = pl.program_id(1)
    seg = pl.program_id(2)
    nch = ts // CHUNK
    dk = LANES
    w = hg * dk

    @pl.when(seg == 0)
    def _():
        state_ref[...] = jnp.zeros_like(state_ref)
        stateb_ref[...] = jnp.zeros_like(stateb_ref)
        for halo in (qh_ref, kh_ref, vh_ref):
            halo[...] = jnp.zeros_like(halo)

    ba = ba_ref[0]
    beta_c = jax.nn.sigmoid(ba)
    g_c = -jnp.exp(alog_ref[...]) * jax.nn.softplus(ba + dtb_ref[...])
    sel_row = lax.broadcasted_iota(jnp.int32, (LANES, w), 0)
    sel_head = lax.broadcasted_iota(jnp.int32, (LANES, w), 1) // dk + grp * hg
    beta_s[...] = _exact_dot_r(beta_c, sel_row == sel_head)

    ri = lax.broadcasted_iota(jnp.int32, (CHUNK, CHUNK), 0)
    ci = lax.broadcasted_iota(jnp.int32, (CHUNK, CHUNK), 1)
    tri = ri >= ci
    strict = ri > ci
    eye = (ri == ci).astype(F32)
    eye_w = (lax.broadcasted_iota(jnp.int32, (CHUNK, w), 1) % dk
             == lax.broadcasted_iota(jnp.int32, (CHUNK, w), 0))
    ones = jnp.ones((CHUNK, CHUNK), F32)
    gc_c = jnp.concatenate([_exact_dot_l(tri, g_c[c * CHUNK:(c + 1) * CHUNK]) for c in range(nch)], axis=0)
    gcol_s[...] = _exact_dot_r(gc_c, sel_row == sel_head + heads)
    for c in range(nch):
        rows = slice(c * CHUNK, (c + 1) * CHUNK)
        grow_s[rows, :] = _exact_dot_l(ones, jnp.where(eye_w, gcol_s[rows, :], 0.0))

    def conv_window(raw_ref, halo_ref, c, r, cols):
        back = pl.multiple_of(jnp.maximum(r - 8, 0), 8)
        prev = jnp.where(c == 0, halo_ref[:, cols], raw_ref[0, pl.ds(back, 8), cols])
        return jnp.concatenate([prev, raw_ref[0, pl.ds(r, CHUNK), cols]], axis=0)

    def conv_silu(win, cw_ref, cols):
        width = cw_ref.shape[0]
        acc = None
        for j in range(width):
            back = width - 1 - j
            shifted = pltpu.roll(win, back, axis=0) if back else win
            term = shifted[8:] * cw_ref[j:j + 1, cols]
            acc = term if acc is None else acc + term
        return _silu(acc)

    def l2n(t):
        return t * lax.rsqrt(jnp.sum(t * t, axis=-1, keepdims=True) + EPS)

    head_cols = [slice(hh * dk, (hh + 1) * dk) for hh in range(hg)]
    half_cols = [slice(hh * dk, hh * dk + CHUNK) for hh in range(hg)]

    def rows_of(c):
        return pl.ds(pl.multiple_of(c * CHUNK, CHUNK), CHUNK)

    def prepare(c, heads_todo):
        r = pl.multiple_of(c * CHUNK, CHUNK)
        r2 = pl.multiple_of(c * (2 * CHUNK), 2 * CHUNK)
        r8 = pl.multiple_of(c * 8, 8)
        for hh in heads_todo:
            cols, half = head_cols[hh], half_cols[hh]
            q = l2n(conv_silu(conv_window(q_ref, qh_ref, c, r, cols), cwq_ref, cols)) * (dk ** -0.5)
            k = l2n(conv_silu(conv_window(k_ref, kh_ref, c, r, cols), cwk_ref, cols))
            v = conv_silu(conv_window(v_ref, vh_ref, c, r, cols), cwv_ref, cols)
            beta = beta_s[pl.ds(r, CHUNK), cols]
            gcol = gcol_s[pl.ds(r, CHUNK), cols]
            decay = jnp.exp(jnp.where(tri, gcol[:, :CHUNK] - grow_s[pl.ds(r, CHUNK), half], -jnp.inf))
            kb = k * beta
            eg = jnp.exp(gcol)
            gl = gcol[CHUNK - 1:CHUNK, :]
            npw_s[pl.ds(r, CHUNK), half] = -jnp.where(strict, _dot_nt(kb, k) * decay, 0.0)
            vb_s[pl.ds(r, CHUNK), cols] = (v * beta).astype(BF16)
            kbe_s[pl.ds(r, CHUNK), cols] = (kb * eg).astype(BF16)
            wq_s[pl.ds(r2 + CHUNK, CHUNK), cols] = (q * eg).astype(BF16)
            kt_s[pl.ds(r, CHUNK), cols] = (k * jnp.exp(gl - gcol)).astype(BF16)
            attn_s[pl.ds(r, CHUNK), half] = (_dot_nt(q, k) * decay).astype(BF16)
            gl_s[pl.ds(r8, 8), cols] = jnp.broadcast_to(jnp.exp(gl), (8, dk))

    def load_npw(c):
        return [npw_s[rows_of(c), half] for half in half_cols]

    def solve_stages(c, npw):
        r = pl.multiple_of(c * CHUNK, CHUNK)
        r2 = pl.multiple_of(c * (2 * CHUNK), 2 * CHUNK)
        t_inv = [eye + p for p in npw]
        pw = [_bdot(p, p) for p in npw]
        yield
        for _ in range(CHUNK.bit_length() - 3):
            pw_b = [p.astype(BF16) for p in pw]
            t_inv = [t + jnp.dot(p, t.astype(BF16), preferred_element_type=F32) for p, t in zip(pw_b, t_inv)]
            pw = [jnp.dot(p, p, preferred_element_type=F32) for p in pw_b]
            yield
        t_inv = [t + _bdot(p, t) for p, t in zip(pw, t_inv)]
        yield
        for cols, t in zip(head_cols, t_inv):
            rhs = jnp.concatenate([vb_s[pl.ds(r, CHUNK), cols], kbe_s[pl.ds(r, CHUNK), cols]], axis=1)
            sol = jnp.dot(t.astype(BF16), rhs, preferred_element_type=F32)
            u_s[pl.ds(r, CHUNK), cols] = sol[:, :dk]
            wq_s[pl.ds(r2, CHUNK), cols] = sol[:, dk:].astype(BF16)

    prepare(0, range(hg))

    def chunk_a(c, carry):
        stages = solve_stages(c - 1, load_npw(c - 1))
        for hh in range(hg):
            next(stages, None)
            prepare(c, [hh])
        for _ in stages:
            pass
        return carry

    lax.fori_loop(1, nch, chunk_a, 0)
    for _ in solve_stages(nch - 1, load_npw(nch - 1)):
        pass
    for halo, raw in ((qh_ref, q_ref), (kh_ref, k_ref), (vh_ref, v_ref)):
        halo[...] = raw[0, ts - 8:ts, :]

    def chunk_b(c, carry):
        r = pl.multiple_of(c * CHUNK, CHUNK)
        r2 = pl.multiple_of(c * (2 * CHUNK), 2 * CHUNK)
        r8 = pl.multiple_of(c * 8, 8)
        hs = range(hg)
        t = [jnp.dot(wq_s[pl.ds(r2, 2 * CHUNK), head_cols[h]], stateb_ref[h], preferred_element_type=F32)
             for h in hs]
        v_new = [(u_s[pl.ds(r, CHUNK), head_cols[h]] - t[h][:CHUNK]).astype(BF16) for h in hs]
        o = [t[h][CHUNK:] + jnp.dot(attn_s[pl.ds(r, CHUNK), half_cols[h]], v_new[h], preferred_element_type=F32)
             for h in hs]
        upd = [_dot_tn(kt_s[pl.ds(r, CHUNK), head_cols[h]], v_new[h]) for h in hs]
        for h in hs:
            s_new = state_ref[h] * gl_s[pl.ds(r8, 1), head_cols[h]] + upd[h]
            state_ref[h] = s_new
            stateb_ref[h] = s_new.astype(BF16)
            gated = _rms(o[h], onorm_ref[...]) * _silu(z_ref[0, pl.ds(r, CHUNK), head_cols[h]])
            o_ref[0, pl.ds(r, CHUNK), head_cols[h]] = gated.astype(o_ref.dtype)
        return carry

    lax.fori_loop(0, nch, chunk_b, 0)


def _gdn(p3, ba3, conv_w, a_log, dt_bias, o_norm, heads, hg, ts):
    bsz, t, _ = p3.shape
    dk = LANES
    w = hg * dk
    ngrp = heads // hg
    nch = ts // CHUNK
    col = lambda part: pl.BlockSpec((1, ts, w), lambda b, g, s: (b, s, part * ngrp + g))
    cw = lambda part: pl.BlockSpec((conv_w.shape[0], w), lambda b, g, s: (0, part * ngrp + g))
    vec = pl.BlockSpec((1, LANES), lambda b, g, s: (0, 0))
    gate_lanes = lambda a: jnp.pad(a, (heads, LANES - 2 * heads)).reshape(1, LANES)
    return pl.pallas_call(
        functools.partial(_gdn_kernel, heads=heads, hg=hg, ts=ts),
        grid=(bsz, ngrp, t // ts),
        in_specs=[col(0), col(1), col(2), col(3),
                  pl.BlockSpec((1, ts, LANES), lambda b, g, s: (b, s, 0)),
                  cw(0), cw(1), cw(2), vec, vec, vec],
        out_specs=pl.BlockSpec((1, ts, w), lambda b, g, s: (b, s, g)),
        out_shape=jax.ShapeDtypeStruct((bsz, t, heads * dk), BF16),
        scratch_shapes=[
            pltpu.VMEM((hg, dk, dk), F32), pltpu.VMEM((hg, dk, dk), BF16),
            pltpu.VMEM((8, w), F32), pltpu.VMEM((8, w), F32), pltpu.VMEM((8, w), F32),
            pltpu.VMEM((ts, w), F32),
            pltpu.VMEM((ts, w), F32), pltpu.VMEM((ts, w), F32),
            pltpu.VMEM((ts, w), F32),
            pltpu.VMEM((ts, w), BF16), pltpu.VMEM((ts, w), BF16),
            pltpu.VMEM((ts, w), F32),
            pltpu.VMEM((2 * ts, w), BF16),
            pltpu.VMEM((ts, w), BF16),
            pltpu.VMEM((ts, w), BF16),
            pltpu.VMEM((8 * nch, w), F32),
        ],
        compiler_params=_cparams(3),
        name="gdn",
    )(p3, p3, p3, p3, ba3, conv_w, conv_w, conv_w, gate_lanes(a_log), gate_lanes(dt_bias),
      o_norm.reshape(1, dk))


def _gmlp_kernel(u_ref, v_ref, lng_ref, lnb_ref, ws_ref, bs_ref, o_ref, *, groups, blk):
    tb = u_ref.shape[0]
    pos_i = lax.broadcasted_iota(jnp.int32, (blk, blk), 0) // CHUNK
    pos_j = lax.broadcasted_iota(jnp.int32, (blk, blk), 1) // CHUNK
    mask = pos_j <= pos_i
    for g in range(groups):
        cs = slice(g * LANES, (g + 1) * LANES)
        vg = _gelu(v_ref[:, cs])
        mu = jnp.mean(vg, axis=-1, keepdims=True)
        vc = vg - mu
        vn = vc * lax.rsqrt(jnp.mean(vc * vc, axis=-1, keepdims=True) + EPS)
        vn = (vn * lng_ref[:, cs] + lnb_ref[:, cs]).astype(BF16)
        wm = jnp.where(mask, ws_ref[g], 0.0).astype(BF16)
        for m in range(tb // blk):
            rs = slice(m * blk, (m + 1) * blk)
            mixed = jnp.dot(wm, vn[rs], preferred_element_type=F32) + bs_ref[g]
            u = _gelu(u_ref[rs, cs])
            o_ref[rs, cs] = (u * mixed).astype(o_ref.dtype)


def _gmlp(p2, u_blk, v_blk, ln_g, ln_b, w_s, bs_b, tb):
    m = p2.shape[0]
    groups, blk, _ = w_s.shape
    w = groups * LANES
    return pl.pallas_call(
        functools.partial(_gmlp_kernel, groups=groups, blk=blk),
        grid=(m // tb,),
        in_specs=[
            pl.BlockSpec((tb, w), lambda i: (i, u_blk)),
            pl.BlockSpec((tb, w), lambda i: (i, v_blk)),
            pl.BlockSpec((1, w), lambda i: (0, 0)),
            pl.BlockSpec((1, w), lambda i: (0, 0)),
            pl.BlockSpec((groups, blk, blk), lambda i: (0, 0, 0)),
            pl.BlockSpec((groups, blk, LANES), lambda i: (0, 0, 0)),
        ],
        out_specs=pl.BlockSpec((tb, w), lambda i: (i, 0)),
        out_shape=jax.ShapeDtypeStruct((m, w), BF16),
        compiler_params=_cparams(1),
        name="gmlp",
    )(p2, p2, ln_g.reshape(1, w), ln_b.reshape(1, w), w_s, bs_b)


def _outproj_kernel(x_ref, a_ref, b_ref, wa_ref, wb_ref, o_ref):
    o_ref[...] = (x_ref[...]
                  + jnp.dot(a_ref[...], wa_ref[...], preferred_element_type=F32)
                  + jnp.dot(b_ref[...], wb_ref[...], preferred_element_type=F32))


def _outproj(x2, oa, ob, w_out, bm, bn):
    m, d = x2.shape
    ka = oa.shape[1]
    kb = ob.shape[1]
    return pl.pallas_call(
        _outproj_kernel,
        grid=(m // bm, d // bn),
        in_specs=[
            pl.BlockSpec((bm, bn), lambda i, j: (i, j)),
            pl.BlockSpec((bm, ka), lambda i, j: (i, 0)),
            pl.BlockSpec((bm, kb), lambda i, j: (i, 0)),
            pl.BlockSpec((ka, bn), lambda i, j: (0, j)),
            pl.BlockSpec((kb, bn), lambda i, j: (ka // kb, j)),
        ],
        out_specs=pl.BlockSpec((bm, bn), lambda i, j: (i, j)),
        out_shape=jax.ShapeDtypeStruct((m, d), F32),
        compiler_params=_cparams(2),
        name="outproj",
    )(x2, oa, ob, w_out, w_out)


def _mlp_kernel(x_ref, g_ref, w1_ref, w2_ref, fg_ref, o_ref, xn_ref, *, final_norm):
    k = pl.program_id(1)

    @pl.when(k == 0)
    def _():
        x = x_ref[...]
        xn_ref[...] = _rms(x, g_ref[...]).astype(BF16)
        o_ref[...] = x

    a = jnp.maximum(jnp.dot(xn_ref[...], w1_ref[0], preferred_element_type=F32), 0.0)
    o_ref[...] += jnp.dot((a * a).astype(BF16), w2_ref[0], preferred_element_type=F32)

    if final_norm:
        @pl.when(k == pl.num_programs(1) - 1)
        def _():
            o_ref[...] = _rms(o_ref[...], fg_ref[...])


def _mlp(x2, gain, w1, w2, layer, final_gain, bm, bf, final_norm):
    m, d = x2.shape
    f = w1.shape[2]
    return pl.pallas_call(
        functools.partial(_mlp_kernel, final_norm=final_norm),
        grid=(m // bm, f // bf),
        in_specs=[
            pl.BlockSpec((bm, d), lambda i, k: (i, 0)),
            pl.BlockSpec((1, d), lambda i, k: (0, 0)),
            pl.BlockSpec((1, d, bf), lambda i, k: (layer, 0, k)),
            pl.BlockSpec((1, bf, d), lambda i, k: (layer, k, 0)),
            pl.BlockSpec((1, d), lambda i, k: (0, 0)),
        ],
        out_specs=pl.BlockSpec((bm, d), lambda i, k: (i, 0)),
        out_shape=jax.ShapeDtypeStruct((m, d), F32),
        scratch_shapes=[pltpu.VMEM((bm, d), BF16)],
        compiler_params=_cparams(2),
        name="mlp_final" if final_norm else "mlp",
    )(x2, gain.reshape(1, d), w1, w2, final_gain.reshape(1, d))


def _gluconv_kernel(x_ref, g_ref, w_ref, b_ref, dw_ref, dwb_ref, lng_ref, lnb_ref, o_ref,
                    xn_ref, cur, prev, zc_ref, *, width, tiles_per_seq, nchunk):
    s = pl.program_id(0)
    bm, d = x_ref.shape
    rb = CHUNK

    @pl.when(s == 0)
    def _():
        cur[...] = jnp.zeros_like(cur)
        prev[...] = jnp.zeros_like(prev)

    starts_sequence = ((s - 1) % tiles_per_seq) == 0
    prev[0:HALO, :] = jnp.where(starts_sequence, 0.0, prev[bm:bm + HALO, :])
    prev[HALO:HALO + bm, :] = cur[...]
    xn_ref[...] = _rms(x_ref[...], g_ref[...]).astype(BF16)

    base = HALO - (width - 1)

    def conv_block(r):
        for c0 in range(0, d, LANES):
            cols = slice(c0, c0 + LANES)
            out = jnp.broadcast_to(dwb_ref[:, cols], (rb, LANES))
            for ph in range(8):
                part = None
                for j in range(width):
                    if (base + j) % 8 != ph:
                        continue
                    lo = r + base + j - ph
                    win = prev[lo:lo + rb + (8 if ph else 0), cols]
                    term = win * dw_ref[j:j + 1, cols]
                    part = term if part is None else part + term
                if part is not None:
                    out = out + part[ph:ph + rb]
            zc_ref[:, cols] = out
        for r0 in range(0, rb, 16):
            acc = zc_ref[r0:r0 + 16, :]
            mu = jnp.mean(acc, axis=-1, keepdims=True)
            xc = acc - mu
            y = xc * lax.rsqrt(jnp.mean(xc * xc, axis=-1, keepdims=True) + EPS)
            y = _silu(y * lng_ref[...] + lnb_ref[...])
            o_ref[r + r0:r + r0 + 16, :] = y.astype(o_ref.dtype)

    cw = d // nchunk
    blocks_per_chunk = (bm // rb) // nchunk
    for c in range(nchunk):
        lin = slice(c * cw, (c + 1) * cw)
        gate = slice(d + c * cw, d + (c + 1) * cw)
        a = jnp.dot(xn_ref[...], w_ref[:, lin], preferred_element_type=F32) + b_ref[:, lin]
        b = jnp.dot(xn_ref[...], w_ref[:, gate], preferred_element_type=F32) + b_ref[:, gate]
        cur[:, lin] = a * jax.nn.sigmoid(b)
        for blk in range(c * blocks_per_chunk, (c + 1) * blocks_per_chunk):
            conv_block(blk * rb)


def _gluconv(x2, gain, pw1, pw1_b, dw, dw_b, ln_g, ln_b, seq, bm, nchunk):
    m, d = x2.shape
    width = dw.shape[0]
    n_tiles = m // bm
    row = lambda a: a.reshape(1, -1)
    const = lambda shape: pl.BlockSpec(shape, lambda s: (0, 0))
    return pl.pallas_call(
        functools.partial(_gluconv_kernel, width=width, tiles_per_seq=seq // bm, nchunk=nchunk),
        grid=(n_tiles + 1,),
        in_specs=[
            pl.BlockSpec((bm, d), lambda s: (jnp.minimum(s, n_tiles - 1), 0)),
            const((1, d)),
            pl.BlockSpec((d, 2 * d), lambda s: (0, 0), pipeline_mode=pl.Buffered(1)),
            const((1, 2 * d)), const((width, d)), const((1, d)), const((1, d)), const((1, d)),
        ],
        out_specs=pl.BlockSpec((bm, d), lambda s: (jnp.maximum(s - 1, 0), 0)),
        out_shape=jax.ShapeDtypeStruct((m, d), BF16),
        scratch_shapes=[pltpu.VMEM((bm, d), BF16), pltpu.VMEM((bm, d), F32), pltpu.VMEM((HALO + bm, d), F32),
                        pltpu.VMEM((CHUNK, d), F32)],
        compiler_params=_cparams(1),
        name="gluconv",
    )(x2, row(gain), pw1, row(pw1_b), dw, row(dw_b), row(ln_g), row(ln_b))


def _pw2_kernel(x_ref, z_ref, w_ref, b_ref, o_ref):
    o_ref[...] = x_ref[...] + jnp.dot(z_ref[...], w_ref[...], preferred_element_type=F32) + b_ref[...]


def _pw2(x2, zs, pw2, pw2_b, bm):
    m, d = x2.shape
    return pl.pallas_call(
        _pw2_kernel,
        grid=(m // bm,),
        in_specs=[
            pl.BlockSpec((bm, d), lambda i: (i, 0)),
            pl.BlockSpec((bm, d), lambda i: (i, 0)),
            pl.BlockSpec((d, d), lambda i: (0, 0)),
            pl.BlockSpec((1, d), lambda i: (0, 0)),
        ],
        out_specs=pl.BlockSpec((bm, d), lambda i: (i, 0)),
        out_shape=jax.ShapeDtypeStruct((m, d), F32),
        compiler_params=_cparams(1),
        name="pw2",
    )(x2, zs, pw2, pw2_b.reshape(1, d))


def _tiles(rows, seq, d, d_ff, in_cols, heads):
    return dict(
        inproj=dict(bm=min(1024, rows), bn=min(1024, in_cols)),
        gdn=dict(hg=min(8, heads), ts=min(512, seq)),
        gmlp=dict(tb=min(256, rows)),
        outproj=dict(bm=min(512, rows), bn=d),
        mlp=dict(bm=min(512, rows), bf=min(1024, d_ff)),
        gluconv=dict(bm=min(512, seq), nchunk=max(1, d // 256)),
        pw2=dict(bm=min(512, rows)),
    )


def _even_layer(x2, bsz, seq, tiles, norm_g, w_in, conv_w, a_log, dt_bias, o_norm, ln_g, ln_b, w_s, b_s, w_out):
    heads = a_log.shape[0]
    groups = w_s.shape[0]
    qkvz = 4 * heads * LANES
    gate0 = qkvz
    uv0 = gate0 + 2 * heads
    w_bf = w_in.astype(BF16)
    w_ba = jnp.pad(w_bf[:, gate0:uv0], ((0, 0), (0, LANES - 2 * heads)))
    p2, ba2 = _inproj(x2, norm_g, w_bf, qkvz, w_bf[:, uv0:], w_ba, **tiles["inproj"])
    p3 = p2.reshape(bsz, seq, -1)
    out_a = _gdn(p3, ba2.reshape(bsz, seq, LANES), conv_w, a_log, dt_bias, o_norm, heads, **tiles["gdn"])
    gw = groups * LANES
    bs_b = jnp.broadcast_to(b_s[:, :, None], b_s.shape + (LANES,))
    out_b = _gmlp(p2, qkvz // gw, qkvz // gw + 1, ln_g, ln_b, w_s, bs_b, **tiles["gmlp"])
    return _outproj(x2, out_a.reshape(bsz * seq, -1), out_b, w_out.astype(BF16), **tiles["outproj"])


def _odd_layer(x2, seq, tiles, norm_g, pw1, pw1_b, dw, dw_b, ln_g, ln_b, pw2, pw2_b):
    zs = _gluconv(x2, norm_g, pw1.astype(BF16), pw1_b, dw, dw_b, ln_g, ln_b, seq, **tiles["gluconv"])
    return _pw2(x2, zs, pw2.astype(BF16), pw2_b, **tiles["pw2"])


def kernel(x, e_norm, e_w_in, e_conv_w, e_a_log, e_dt_bias, e_o_norm, e_ln_g, e_ln_b, e_w_s, e_b_s, e_w_out, o_norm, o_pw1, o_pw1_b, o_dw, o_dw_b, o_ln_g, o_ln_b, o_pw2, o_pw2_b, f_norm, f_w1, f_w2, final_norm):
    bsz, seq, d = x.shape
    depth = f_norm.shape[0]
    x2 = x.reshape(bsz * seq, d)
    heads = e_a_log.shape[1]
    tiles = _tiles(bsz * seq, seq, d, f_w1.shape[2], e_w_in.shape[2] - 2 * heads, heads)
    w1_bf = f_w1.astype(BF16)
    w2_bf = f_w2.astype(BF16)
    for layer in range(depth):
        i = layer // 2
        if layer % 2 == 0:
            x2 = _even_layer(x2, bsz, seq, tiles, e_norm[i], e_w_in[i], e_conv_w[i], e_a_log[i], e_dt_bias[i],
                             e_o_norm[i], e_ln_g[i], e_ln_b[i], e_w_s[i], e_b_s[i], e_w_out[i])
        else:
            x2 = _odd_layer(x2, seq, tiles, o_norm[i], o_pw1[i], o_pw1_b[i], o_dw[i], o_dw_b[i], o_ln_g[i],
                            o_ln_b[i], o_pw2[i], o_pw2_b[i])
        x2 = _mlp(x2, f_norm[layer], w1_bf, w2_bf, layer, final_norm,
                  final_norm=(layer == depth - 1), **tiles["mlp"])
    return x2.reshape(bsz, seq, d)
```

```python
import functools

import jax
import jax.numpy as jnp
from jax import lax
from jax.experimental import pallas as pl
from jax.experimental.pallas import tpu as pltpu

F32 = jnp.float32
BF16 = jnp.bfloat16
EPS = 1e-6
CHUNK = 64
LANES = 128
HALO = 32
VMEM_LIMIT = 56 * 1024 * 1024
HI = lax.Precision.HIGHEST


def _cparams(n_axes):
    return pltpu.CompilerParams(dimension_semantics=("arbitrary",) * n_axes,
                                vmem_limit_bytes=VMEM_LIMIT)


def _bdot(a, b):
    return jnp.dot(a.astype(BF16), b.astype(BF16), preferred_element_type=F32)


def _dot_nt(a, b):
    return lax.dot_general(a.astype(BF16), b.astype(BF16), (((1,), (1,)), ((), ())),
                           preferred_element_type=F32)


def _dot_tn(a, b):
    return lax.dot_general(a.astype(BF16), b.astype(BF16), (((0,), (0,)), ((), ())),
                           preferred_element_type=F32)


def _rms(x, g):
    return x * lax.rsqrt(jnp.mean(x * x, axis=-1, keepdims=True) + EPS) * g


def _silu(x):
    return x * jax.nn.sigmoid(x)


def _gelu(x):
    return 0.5 * x * (1.0 + lax.erf(x * (2.0 ** -0.5)))


def _inproj_kernel(x_ref, g_ref, wa_ref, wb_ref, wba_ref, p_ref, ba_ref, xn_ref, *, na):
    j = pl.program_id(1)

    @pl.when(j == 0)
    def _():
        xn = _rms(x_ref[...], g_ref[...]).astype(BF16)
        xn_ref[...] = xn
        ba_ref[...] = jnp.dot(xn, wba_ref[...], preferred_element_type=F32)

    @pl.when(j < na)
    def _():
        p_ref[...] = jnp.dot(xn_ref[...], wa_ref[...], preferred_element_type=F32)

    @pl.when(j >= na)
    def _():
        p_ref[...] = jnp.dot(xn_ref[...], wb_ref[...], preferred_element_type=F32)


def _inproj(x2, gain, w_a, n_a, w_b, w_ba, bm, bn):
    m, d = x2.shape
    na = n_a // bn
    nb = w_b.shape[1] // bn
    n = n_a + w_b.shape[1]
    return pl.pallas_call(
        functools.partial(_inproj_kernel, na=na),
        grid=(m // bm, na + nb),
        in_specs=[
            pl.BlockSpec((bm, d), lambda i, j: (i, 0)),
            pl.BlockSpec((1, d), lambda i, j: (0, 0)),
            pl.BlockSpec((d, bn), lambda i, j: (0, jnp.minimum(j, na - 1))),
            pl.BlockSpec((d, bn), lambda i, j: (0, jnp.maximum(j - na, 0))),
            pl.BlockSpec((d, LANES), lambda i, j: (0, 0)),
        ],
        out_specs=[
            pl.BlockSpec((bm, bn), lambda i, j: (i, j)),
            pl.BlockSpec((bm, LANES), lambda i, j: (i, 0)),
        ],
        out_shape=[jax.ShapeDtypeStruct((m, n), F32), jax.ShapeDtypeStruct((m, LANES), F32)],
        scratch_shapes=[pltpu.VMEM((bm, d), BF16)],
        compiler_params=_cparams(2),
        name="inproj",
    )(x2, gain.reshape(1, d), w_a, w_b, w_ba)


def _split3(x):
    h1 = x.astype(BF16)
    r1 = x - h1.astype(F32)
    h2 = r1.astype(BF16)
    h3 = (r1 - h2.astype(F32)).astype(BF16)
    return h1, h2, h3


def _exact_dot_l(m01, x):
    m = m01.astype(BF16)
    h1, h2, h3 = _split3(x)
    return (jnp.dot(m, h3, preferred_element_type=F32) + jnp.dot(m, h2, preferred_element_type=F32)
            + jnp.dot(m, h1, preferred_element_type=F32))


def _exact_dot_r(x, m01):
    m = m01.astype(BF16)
    h1, h2, h3 = _split3(x)
    return (jnp.dot(h3, m, preferred_element_type=F32) + jnp.dot(h2, m, preferred_element_type=F32)
            + jnp.dot(h1, m, preferred_element_type=F32))


def _gdn_kernel(q_ref, k_ref, v_ref, z_ref, ba_ref, cwq_ref, cwk_ref, cwv_ref, alog_ref, dtb_ref, onorm_ref,
                o_ref,
                state_ref, stateb_ref, qh_ref, kh_ref, vh_ref, beta_s, gcol_s, grow_s, npw_s, vb_s, kbe_s,
                u_s, wq_s, kt_s, attn_s, gl_s, *, heads, hg, ts):
    grp = pl.program_id(1)
    seg = pl.program_id(2)
    nch = ts // CHUNK
    dk = LANES
    w = hg * dk

    @pl.when(seg == 0)
    def _():
        state_ref[...] = jnp.zeros_like(state_ref)
        stateb_ref[...] = jnp.zeros_like(stateb_ref)
        for halo in (qh_ref, kh_ref, vh_ref):
            halo[...] = jnp.zeros_like(halo)

    ba = ba_ref[0]
    beta_c = jax.nn.sigmoid(ba)
    g_c = -jnp.exp(alog_ref[...]) * jax.nn.softplus(ba + dtb_ref[...])
    sel_row = lax.broadcasted_iota(jnp.int32, (LANES, w), 0)
    sel_head = lax.broadcasted_iota(jnp.int32, (LANES, w), 1) // dk + grp * hg
    beta_s[...] = _exact_dot_r(beta_c, sel_row == sel_head)

    ri = lax.broadcasted_iota(jnp.int32, (CHUNK, CHUNK), 0)
    ci = lax.broadcasted_iota(jnp.int32, (CHUNK, CHUNK), 1)
    tri = ri >= ci
    strict = ri > ci
    eye = (ri == ci).astype(F32)
    eye_w = (lax.broadcasted_iota(jnp.int32, (CHUNK, w), 1) % dk
             == lax.broadcasted_iota(jnp.int32, (CHUNK, w), 0))
    ones = jnp.ones((CHUNK, CHUNK), F32)
    gc_c = jnp.concatenate([_exact_dot_l(tri, g_c[c * CHUNK:(c + 1) * CHUNK]) for c in range(nch)], axis=0)
    gcol_s[...] = _exact_dot_r(gc_c, sel_row == sel_head + heads)
    for c in range(nch):
        rows = slice(c * CHUNK, (c + 1) * CHUNK)
        grow_s[rows, :] = _exact_dot_l(ones, jnp.where(eye_w, gcol_s[rows, :], 0.0))

    def conv_window(raw_ref, halo_ref, c, r, cols):
        back = pl.multiple_of(jnp.maximum(r - 8, 0), 8)
        prev = jnp.where(c == 0, halo_ref[:, cols], raw_ref[0, pl.ds(back, 8), cols])
        return jnp.concatenate([prev, raw_ref[0, pl.ds(r, CHUNK), cols]], axis=0)

    def conv_silu(win, cw_ref, cols):
        width = cw_ref.shape[0]
        acc = None
        for j in range(width):
            back = width - 1 - j
            shifted = pltpu.roll(win, back, axis=0) if back else win
            term = shifted[8:] * cw_ref[j:j + 1, cols]
            acc = term if acc is None else acc + term
        return _silu(acc)

    def l2n(t):
        return t * lax.rsqrt(jnp.sum(t * t, axis=-1, keepdims=True) + EPS)

    head_cols = [slice(hh * dk, (hh + 1) * dk) for hh in range(hg)]
    half_cols = [slice(hh * dk, hh * dk + CHUNK) for hh in range(hg)]

    def rows_of(c):
        return pl.ds(pl.multiple_of(c * CHUNK, CHUNK), CHUNK)

    def prepare(c, heads_todo):
        r = pl.multiple_of(c * CHUNK, CHUNK)
        r2 = pl.multiple_of(c * (2 * CHUNK), 2 * CHUNK)
        r8 = pl.multiple_of(c * 8, 8)
        for hh in heads_todo:
            cols, half = head_cols[hh], half_cols[hh]
            q = l2n(conv_silu(conv_window(q_ref, qh_ref, c, r, cols), cwq_ref, cols)) * (dk ** -0.5)
            k = l2n(conv_silu(conv_window(k_ref, kh_ref, c, r, cols), cwk_ref, cols))
            v = conv_silu(conv_window(v_ref, vh_ref, c, r, cols), cwv_ref, cols)
            beta = beta_s[pl.ds(r, CHUNK), cols]
            gcol = gcol_s[pl.ds(r, CHUNK), cols]
            decay = jnp.exp(jnp.where(tri, gcol[:, :CHUNK] - grow_s[pl.ds(r, CHUNK), half], -jnp.inf))
            kb = k * beta
            eg = jnp.exp(gcol)
            gl = gcol[CHUNK - 1:CHUNK, :]
            npw_s[pl.ds(r, CHUNK), half] = -jnp.where(strict, _dot_nt(kb, k) * decay, 0.0)
            vb_s[pl.ds(r, CHUNK), cols] = (v * beta).astype(BF16)
            kbe_s[pl.ds(r, CHUNK), cols] = (kb * eg).astype(BF16)
            wq_s[pl.ds(r2 + CHUNK, CHUNK), cols] = (q * eg).astype(BF16)
            kt_s[pl.ds(r, CHUNK), cols] = (k * jnp.exp(gl - gcol)).astype(BF16)
            attn_s[pl.ds(r, CHUNK), half] = (_dot_nt(q, k) * decay).astype(BF16)
            gl_s[pl.ds(r8, 8), cols] = jnp.broadcast_to(jnp.exp(gl), (8, dk))

    def load_npw(c):
        return [npw_s[rows_of(c), half] for half in half_cols]

    def solve_stages(c, npw):
        r = pl.multiple_of(c * CHUNK, CHUNK)
        r2 = pl.multiple_of(c * (2 * CHUNK), 2 * CHUNK)
        t_inv = [eye + p for p in npw]
        pw = [_bdot(p, p) for p in npw]
        yield
        for _ in range(CHUNK.bit_length() - 3):
            pw_b = [p.astype(BF16) for p in pw]
            t_inv = [t + jnp.dot(p, t.astype(BF16), preferred_element_type=F32) for p, t in zip(pw_b, t_inv)]
            pw = [jnp.dot(p, p, preferred_element_type=F32) for p in pw_b]
            yield
        t_inv = [t + _bdot(p, t) for p, t in zip(pw, t_inv)]
        yield
        for cols, t in zip(head_cols, t_inv):
            rhs = jnp.concatenate([vb_s[pl.ds(r, CHUNK), cols], kbe_s[pl.ds(r, CHUNK), cols]], axis=1)
            sol = jnp.dot(t.astype(BF16), rhs, preferred_element_type=F32)
            u_s[pl.ds(r, CHUNK), cols] = sol[:, :dk]
            wq_s[pl.ds(r2, CHUNK), cols] = sol[:, dk:].astype(BF16)

    def recur_stages(c):
        r = pl.multiple_of(c * CHUNK, CHUNK)
        r2 = pl.multiple_of(c * (2 * CHUNK), 2 * CHUNK)
        r8 = pl.multiple_of(c * 8, 8)
        hs = range(hg)
        t = [jnp.dot(wq_s[pl.ds(r2, 2 * CHUNK), head_cols[h]], stateb_ref[h], preferred_element_type=F32)
             for h in hs]
        yield
        v_new = [(u_s[pl.ds(r, CHUNK), head_cols[h]] - t[h][:CHUNK]).astype(BF16) for h in hs]
        o = [t[h][CHUNK:] + jnp.dot(attn_s[pl.ds(r, CHUNK), half_cols[h]], v_new[h], preferred_element_type=F32)
             for h in hs]
        upd = [_dot_tn(kt_s[pl.ds(r, CHUNK), head_cols[h]], v_new[h]) for h in hs]
        yield
        for h in hs:
            s_new = state_ref[h] * gl_s[pl.ds(r8, 1), head_cols[h]] + upd[h]
            state_ref[h] = s_new
            stateb_ref[h] = s_new.astype(BF16)
            gated = _rms(o[h], onorm_ref[...]) * _silu(z_ref[0, pl.ds(r, CHUNK), head_cols[h]])
            o_ref[0, pl.ds(r, CHUNK), head_cols[h]] = gated.astype(o_ref.dtype)

    def pipeline_body(prep_c=None, solve_c=None, recur_c=None):
        staged = []
        if solve_c is not None:
            staged.append(solve_stages(solve_c, load_npw(solve_c)))
        if recur_c is not None:
            staged.append(recur_stages(recur_c))
        for hh in range(hg):
            for stages in staged:
                next(stages, None)
            if prep_c is not None:
                prepare(prep_c, [hh])
        for stages in staged:
            for _ in stages:
                pass

    def steady(c, carry):
        pipeline_body(c, c - 1, c - 2)
        return carry

    pipeline_body(prep_c=0)
    if nch > 1:
        pipeline_body(prep_c=1, solve_c=0)
    lax.fori_loop(2, nch, steady, 0)
    pipeline_body(solve_c=nch - 1, recur_c=nch - 2 if nch > 1 else None)
    pipeline_body(recur_c=nch - 1)
    for halo, raw in ((qh_ref, q_ref), (kh_ref, k_ref), (vh_ref, v_ref)):
        halo[...] = raw[0, ts - 8:ts, :]


def _gdn(p3, ba3, conv_w, a_log, dt_bias, o_norm, heads, hg, ts):
    bsz, t, _ = p3.shape
    dk = LANES
    w = hg * dk
    ngrp = heads // hg
    nch = ts // CHUNK
    col = lambda part: pl.BlockSpec((1, ts, w), lambda b, g, s: (b, s, part * ngrp + g))
    cw = lambda part: pl.BlockSpec((conv_w.shape[0], w), lambda b, g, s: (0, part * ngrp + g))
    vec = pl.BlockSpec((1, LANES), lambda b, g, s: (0, 0))
    gate_lanes = lambda a: jnp.pad(a, (heads, LANES - 2 * heads)).reshape(1, LANES)
    return pl.pallas_call(
        functools.partial(_gdn_kernel, heads=heads, hg=hg, ts=ts),
        grid=(bsz, ngrp, t // ts),
        in_specs=[col(0), col(1), col(2), col(3),
                  pl.BlockSpec((1, ts, LANES), lambda b, g, s: (b, s, 0)),
                  cw(0), cw(1), cw(2), vec, vec, vec],
        out_specs=pl.BlockSpec((1, ts, w), lambda b, g, s: (b, s, g)),
        out_shape=jax.ShapeDtypeStruct((bsz, t, heads * dk), BF16),
        scratch_shapes=[
            pltpu.VMEM((hg, dk, dk), F32), pltpu.VMEM((hg, dk, dk), BF16),
            pltpu.VMEM((8, w), F32), pltpu.VMEM((8, w), F32), pltpu.VMEM((8, w), F32),
            pltpu.VMEM((ts, w), F32),
            pltpu.VMEM((ts, w), F32), pltpu.VMEM((ts, w), F32),
            pltpu.VMEM((ts, w), F32),
            pltpu.VMEM((ts, w), BF16), pltpu.VMEM((ts, w), BF16),
            pltpu.VMEM((ts, w), F32),
            pltpu.VMEM((2 * ts, w), BF16),
            pltpu.VMEM((ts, w), BF16),
            pltpu.VMEM((ts, w), BF16),
            pltpu.VMEM((8 * nch, w), F32),
        ],
        compiler_params=_cparams(3),
        name="gdn",
    )(p3, p3, p3, p3, ba3, conv_w, conv_w, conv_w, gate_lanes(a_log), gate_lanes(dt_bias),
      o_norm.reshape(1, dk))


def _gmlp_kernel(u_ref, v_ref, lng_ref, lnb_ref, ws_ref, bs_ref, o_ref, *, groups, blk):
    tb = u_ref.shape[0]
    pos_i = lax.broadcasted_iota(jnp.int32, (blk, blk), 0) // CHUNK
    pos_j = lax.broadcasted_iota(jnp.int32, (blk, blk), 1) // CHUNK
    mask = pos_j <= pos_i
    for g in range(groups):
        cs = slice(g * LANES, (g + 1) * LANES)
        vg = _gelu(v_ref[:, cs])
        mu = jnp.mean(vg, axis=-1, keepdims=True)
        vc = vg - mu
        vn = vc * lax.rsqrt(jnp.mean(vc * vc, axis=-1, keepdims=True) + EPS)
        vn = (vn * lng_ref[:, cs] + lnb_ref[:, cs]).astype(BF16)
        wm = jnp.where(mask, ws_ref[g], 0.0).astype(BF16)
        for m in range(tb // blk):
            rs = slice(m * blk, (m + 1) * blk)
            mixed = jnp.dot(wm, vn[rs], preferred_element_type=F32) + bs_ref[g]
            u = _gelu(u_ref[rs, cs])
            o_ref[rs, cs] = (u * mixed).astype(o_ref.dtype)


def _gmlp(p2, u_blk, v_blk, ln_g, ln_b, w_s, bs_b, tb):
    m = p2.shape[0]
    groups, blk, _ = w_s.shape
    w = groups * LANES
    return pl.pallas_call(
        functools.partial(_gmlp_kernel, groups=groups, blk=blk),
        grid=(m // tb,),
        in_specs=[
            pl.BlockSpec((tb, w), lambda i: (i, u_blk)),
            pl.BlockSpec((tb, w), lambda i: (i, v_blk)),
            pl.BlockSpec((1, w), lambda i: (0, 0)),
            pl.BlockSpec((1, w), lambda i: (0, 0)),
            pl.BlockSpec((groups, blk, blk), lambda i: (0, 0, 0)),
            pl.BlockSpec((groups, blk, LANES), lambda i: (0, 0, 0)),
        ],
        out_specs=pl.BlockSpec((tb, w), lambda i: (i, 0)),
        out_shape=jax.ShapeDtypeStruct((m, w), BF16),
        compiler_params=_cparams(1),
        name="gmlp",
    )(p2, p2, ln_g.reshape(1, w), ln_b.reshape(1, w), w_s, bs_b)


def _outproj_kernel(x_ref, a_ref, b_ref, wa_ref, wb_ref, o_ref):
    o_ref[...] = (x_ref[...]
                  + jnp.dot(a_ref[...], wa_ref[...], preferred_element_type=F32)
                  + jnp.dot(b_ref[...], wb_ref[...], preferred_element_type=F32))


def _outproj(x2, oa, ob, w_out, bm, bn):
    m, d = x2.shape
    ka = oa.shape[1]
    kb = ob.shape[1]
    return pl.pallas_call(
        _outproj_kernel,
        grid=(m // bm, d // bn),
        in_specs=[
            pl.BlockSpec((bm, bn), lambda i, j: (i, j)),
            pl.BlockSpec((bm, ka), lambda i, j: (i, 0)),
            pl.BlockSpec((bm, kb), lambda i, j: (i, 0)),
            pl.BlockSpec((ka, bn), lambda i, j: (0, j)),
            pl.BlockSpec((kb, bn), lambda i, j: (ka // kb, j)),
        ],
        out_specs=pl.BlockSpec((bm, bn), lambda i, j: (i, j)),
        out_shape=jax.ShapeDtypeStruct((m, d), F32),
        compiler_params=_cparams(2),
        name="outproj",
    )(x2, oa, ob, w_out, w_out)


def _mlp_kernel(x_ref, g_ref, w1_ref, w2_ref, fg_ref, o_ref, xn_ref, *, final_norm):
    k = pl.program_id(1)

    @pl.when(k == 0)
    def _():
        x = x_ref[...]
        xn_ref[...] = _rms(x, g_ref[...]).astype(BF16)
        o_ref[...] = x

    a = jnp.maximum(jnp.dot(xn_ref[...], w1_ref[0], preferred_element_type=F32), 0.0)
    o_ref[...] += jnp.dot((a * a).astype(BF16), w2_ref[0], preferred_element_type=F32)

    if final_norm:
        @pl.when(k == pl.num_programs(1) - 1)
        def _():
            o_ref[...] = _rms(o_ref[...], fg_ref[...])


def _mlp(x2, gain, w1, w2, layer, final_gain, bm, bf, final_norm):
    m, d = x2.shape
    f = w1.shape[2]
    return pl.pallas_call(
        functools.partial(_mlp_kernel, final_norm=final_norm),
        grid=(m // bm, f // bf),
        in_specs=[
            pl.BlockSpec((bm, d), lambda i, k: (i, 0)),
            pl.BlockSpec((1, d), lambda i, k: (0, 0)),
            pl.BlockSpec((1, d, bf), lambda i, k: (layer, 0, k)),
            pl.BlockSpec((1, bf, d), lambda i, k: (layer, k, 0)),
            pl.BlockSpec((1, d), lambda i, k: (0, 0)),
        ],
        out_specs=pl.BlockSpec((bm, d), lambda i, k: (i, 0)),
        out_shape=jax.ShapeDtypeStruct((m, d), F32),
        scratch_shapes=[pltpu.VMEM((bm, d), BF16)],
        compiler_params=_cparams(2),
        name="mlp_final" if final_norm else "mlp",
    )(x2, gain.reshape(1, d), w1, w2, final_gain.reshape(1, d))


def _gluconv_kernel(x_ref, g_ref, w_ref, b_ref, dw_ref, dwb_ref, lng_ref, lnb_ref, o_ref,
                    xn_ref, cur, prev, zc_ref, *, width, tiles_per_seq, nchunk):
    s = pl.program_id(0)
    bm, d = x_ref.shape
    rb = CHUNK

    @pl.when(s == 0)
    def _():
        cur[...] = jnp.zeros_like(cur)
        prev[...] = jnp.zeros_like(prev)

    starts_sequence = ((s - 1) % tiles_per_seq) == 0
    prev[0:HALO, :] = jnp.where(starts_sequence, 0.0, prev[bm:bm + HALO, :])
    prev[HALO:HALO + bm, :] = cur[...]
    xn_ref[...] = _rms(x_ref[...], g_ref[...]).astype(BF16)

    base = HALO - (width - 1)

    def conv_block(r):
        for c0 in range(0, d, LANES):
            cols = slice(c0, c0 + LANES)
            out = jnp.broadcast_to(dwb_ref[:, cols], (rb, LANES))
            for ph in range(8):
                part = None
                for j in range(width):
                    if (base + j) % 8 != ph:
                        continue
                    lo = r + base + j - ph
                    win = prev[lo:lo + rb + (8 if ph else 0), cols]
                    term = win * dw_ref[j:j + 1, cols]
                    part = term if part is None else part + term
                if part is not None:
                    if ph:
                        part = pltpu.roll(part, rb + 8 - ph, axis=0)
                    out = out + part[:rb]
            zc_ref[:, cols] = out
        for r0 in range(0, rb, 16):
            acc = zc_ref[r0:r0 + 16, :]
            mu = jnp.mean(acc, axis=-1, keepdims=True)
            xc = acc - mu
            y = xc * lax.rsqrt(jnp.mean(xc * xc, axis=-1, keepdims=True) + EPS)
            y = _silu(y * lng_ref[...] + lnb_ref[...])
            o_ref[r + r0:r + r0 + 16, :] = y.astype(o_ref.dtype)

    cw = d // nchunk
    blocks_per_chunk = (bm // rb) // nchunk
    for c in range(nchunk):
        lin = slice(c * cw, (c + 1) * cw)
        gate = slice(d + c * cw, d + (c + 1) * cw)
        a = jnp.dot(xn_ref[...], w_ref[:, lin], preferred_element_type=F32) + b_ref[:, lin]
        b = jnp.dot(xn_ref[...], w_ref[:, gate], preferred_element_type=F32) + b_ref[:, gate]
        cur[:, lin] = a * jax.nn.sigmoid(b)
        for blk in range(c * blocks_per_chunk, (c + 1) * blocks_per_chunk):
            conv_block(blk * rb)


def _gluconv(x2, gain, pw1, pw1_b, dw, dw_b, ln_g, ln_b, seq, bm, nchunk):
    m, d = x2.shape
    width = dw.shape[0]
    n_tiles = m // bm
    row = lambda a: a.reshape(1, -1)
    const = lambda shape: pl.BlockSpec(shape, lambda s: (0, 0))
    return pl.pallas_call(
        functools.partial(_gluconv_kernel, width=width, tiles_per_seq=seq // bm, nchunk=nchunk),
        grid=(n_tiles + 1,),
        in_specs=[
            pl.BlockSpec((bm, d), lambda s: (jnp.minimum(s, n_tiles - 1), 0)),
            const((1, d)),
            pl.BlockSpec((d, 2 * d), lambda s: (0, 0), pipeline_mode=pl.Buffered(1)),
            const((1, 2 * d)), const((width, d)), const((1, d)), const((1, d)), const((1, d)),
        ],
        out_specs=pl.BlockSpec((bm, d), lambda s: (jnp.maximum(s - 1, 0), 0)),
        out_shape=jax.ShapeDtypeStruct((m, d), BF16),
        scratch_shapes=[pltpu.VMEM((bm, d), BF16), pltpu.VMEM((bm, d), F32), pltpu.VMEM((HALO + bm, d), F32),
                        pltpu.VMEM((CHUNK, d), F32)],
        compiler_params=_cparams(1),
        name="gluconv",
    )(x2, row(gain), pw1, row(pw1_b), dw, row(dw_b), row(ln_g), row(ln_b))


def _pw2_kernel(x_ref, z_ref, w_ref, b_ref, o_ref):
    o_ref[...] = x_ref[...] + jnp.dot(z_ref[...], w_ref[...], preferred_element_type=F32) + b_ref[...]


def _pw2(x2, zs, pw2, pw2_b, bm):
    m, d = x2.shape
    return pl.pallas_call(
        _pw2_kernel,
        grid=(m // bm,),
        in_specs=[
            pl.BlockSpec((bm, d), lambda i: (i, 0)),
            pl.BlockSpec((bm, d), lambda i: (i, 0)),
            pl.BlockSpec((d, d), lambda i: (0, 0)),
            pl.BlockSpec((1, d), lambda i: (0, 0)),
        ],
        out_specs=pl.BlockSpec((bm, d), lambda i: (i, 0)),
        out_shape=jax.ShapeDtypeStruct((m, d), F32),
        compiler_params=_cparams(1),
        name="pw2",
    )(x2, zs, pw2, pw2_b.reshape(1, d))


def _tiles(rows, seq, d, d_ff, in_cols, heads):
    return dict(
        inproj=dict(bm=min(1024, rows), bn=min(1024, in_cols)),
        gdn=dict(hg=min(8, heads), ts=min(512, seq)),
        gmlp=dict(tb=min(256, rows)),
        outproj=dict(bm=min(512, rows), bn=d),
        mlp=dict(bm=min(512, rows), bf=min(2048, d_ff)),
        gluconv=dict(bm=min(512, seq), nchunk=max(1, d // 256)),
        pw2=dict(bm=min(512, rows)),
    )


def _even_layer(x2, bsz, seq, tiles, norm_g, w_in, conv_w, a_log, dt_bias, o_norm, ln_g, ln_b, w_s, b_s, w_out):
    heads = a_log.shape[0]
    groups = w_s.shape[0]
    qkvz = 4 * heads * LANES
    gate0 = qkvz
    uv0 = gate0 + 2 * heads
    w_bf = w_in.astype(BF16)
    w_ba = jnp.pad(w_bf[:, gate0:uv0], ((0, 0), (0, LANES - 2 * heads)))
    p2, ba2 = _inproj(x2, norm_g, w_bf, qkvz, w_bf[:, uv0:], w_ba, **tiles["inproj"])
    p3 = p2.reshape(bsz, seq, -1)
    out_a = _gdn(p3, ba2.reshape(bsz, seq, LANES), conv_w, a_log, dt_bias, o_norm, heads, **tiles["gdn"])
    gw = groups * LANES
    bs_b = jnp.broadcast_to(b_s[:, :, None], b_s.shape + (LANES,))
    out_b = _gmlp(p2, qkvz // gw, qkvz // gw + 1, ln_g, ln_b, w_s, bs_b, **tiles["gmlp"])
    return _outproj(x2, out_a.reshape(bsz * seq, -1), out_b, w_out.astype(BF16), **tiles["outproj"])


def _odd_layer(x2, seq, tiles, norm_g, pw1, pw1_b, dw, dw_b, ln_g, ln_b, pw2, pw2_b):
    zs = _gluconv(x2, norm_g, pw1.astype(BF16), pw1_b, dw, dw_b, ln_g, ln_b, seq, **tiles["gluconv"])
    return _pw2(x2, zs, pw2.astype(BF16), pw2_b, **tiles["pw2"])


def kernel(x, e_norm, e_w_in, e_conv_w, e_a_log, e_dt_bias, e_o_norm, e_ln_g, e_ln_b, e_w_s, e_b_s, e_w_out, o_norm, o_pw1, o_pw1_b, o_dw, o_dw_b, o_ln_g, o_ln_b, o_pw2, o_pw2_b, f_norm, f_w1, f_w2, final_norm):
    bsz, seq, d = x.shape
    depth = f_norm.shape[0]
    x2 = x.reshape(bsz * seq, d)
    heads = e_a_log.shape[1]
    tiles = _tiles(bsz * seq, seq, d, f_w1.shape[2], e_w_in.shape[2] - 2 * heads, heads)
    w1_bf = f_w1.astype(BF16)
    w2_bf = f_w2.astype(BF16)
    for layer in range(depth):
        i = layer // 2
        if layer % 2 == 0:
            x2 = _even_layer(x2, bsz, seq, tiles, e_norm[i], e_w_in[i], e_conv_w[i], e_a_log[i], e_dt_bias[i],
                             e_o_norm[i], e_ln_g[i], e_ln_b[i], e_w_s[i], e_b_s[i], e_w_out[i])
        else:
            x2 = _odd_layer(x2, seq, tiles, o_norm[i], o_pw1[i], o_pw1_b[i], o_dw[i], o_dw_b[i], o_ln_g[i],
                            o_ln_b[i], o_pw2[i], o_pw2_b[i])
        x2 = _mlp(x2, f_norm[layer], w1_bf, w2_bf, layer, final_norm,
                  final_norm=(layer == depth - 1), **tiles["mlp"])
    return x2.reshape(bsz, seq, d)
```

```python
import functools

import jax
import jax.numpy as jnp
from jax import lax
from jax.experimental import pallas as pl
from jax.experimental.pallas import tpu as pltpu

F32 = jnp.float32
BF16 = jnp.bfloat16
EPS = 1e-6
CHUNK = 64
LANES = 128
SUBLANES = 8
BF16_ROWS = 16
MXU_COLS = 256
HALO = 32
VMEM_LIMIT = 56 * 1024 * 1024


def _cparams(n_axes):
    return pltpu.CompilerParams(dimension_semantics=("arbitrary",) * n_axes,
                                vmem_limit_bytes=VMEM_LIMIT)


def _bdot(a, b):
    return jnp.dot(a.astype(BF16), b.astype(BF16), preferred_element_type=F32)


def _dot_nt(a, b):
    return lax.dot_general(a.astype(BF16), b.astype(BF16), (((1,), (1,)), ((), ())),
                           preferred_element_type=F32)


def _dot_tn(a, b):
    return lax.dot_general(a.astype(BF16), b.astype(BF16), (((0,), (0,)), ((), ())),
                           preferred_element_type=F32)


def _rms(x, g):
    return x * lax.rsqrt(jnp.mean(x * x, axis=-1, keepdims=True) + EPS) * g


def _silu(x):
    return x * jax.nn.sigmoid(x)


def _gelu(x):
    return 0.5 * x * (1.0 + lax.erf(x * (2.0 ** -0.5)))


def _inproj_kernel(x_ref, g_ref, wa_ref, wb_ref, wba_ref, p_ref, ba_ref, xn_ref, *, na):
    j = pl.program_id(1)

    @pl.when(j == 0)
    def _():
        xn = _rms(x_ref[...], g_ref[...]).astype(BF16)
        xn_ref[...] = xn
        ba_ref[...] = jnp.dot(xn, wba_ref[...], preferred_element_type=F32)

    @pl.when(j < na)
    def _():
        p_ref[...] = jnp.dot(xn_ref[...], wa_ref[...], preferred_element_type=F32)

    @pl.when(j >= na)
    def _():
        p_ref[...] = jnp.dot(xn_ref[...], wb_ref[...], preferred_element_type=F32)


def _inproj(x2, gain, w_a, n_a, w_b, w_ba, bm, bn):
    m, d = x2.shape
    na = n_a // bn
    nb = w_b.shape[1] // bn
    n = n_a + w_b.shape[1]
    return pl.pallas_call(
        functools.partial(_inproj_kernel, na=na),
        grid=(m // bm, na + nb),
        in_specs=[
            pl.BlockSpec((bm, d), lambda i, j: (i, 0)),
            pl.BlockSpec((1, d), lambda i, j: (0, 0)),
            pl.BlockSpec((d, bn), lambda i, j: (0, jnp.minimum(j, na - 1))),
            pl.BlockSpec((d, bn), lambda i, j: (0, jnp.maximum(j - na, 0))),
            pl.BlockSpec((d, LANES), lambda i, j: (0, 0)),
        ],
        out_specs=[
            pl.BlockSpec((bm, bn), lambda i, j: (i, j)),
            pl.BlockSpec((bm, LANES), lambda i, j: (i, 0)),
        ],
        out_shape=[jax.ShapeDtypeStruct((m, n), F32), jax.ShapeDtypeStruct((m, LANES), F32)],
        scratch_shapes=[pltpu.VMEM((bm, d), BF16)],
        compiler_params=_cparams(2),
        name="inproj",
    )(x2, gain.reshape(1, d), w_a, w_b, w_ba)


def _split3(x):
    h1 = x.astype(BF16)
    r1 = x - h1.astype(F32)
    h2 = r1.astype(BF16)
    h3 = (r1 - h2.astype(F32)).astype(BF16)
    return h1, h2, h3


def _exact_dot_l(m01, x):
    m = m01.astype(BF16)
    h1, h2, h3 = _split3(x)
    return (jnp.dot(m, h3, preferred_element_type=F32) + jnp.dot(m, h2, preferred_element_type=F32)
            + jnp.dot(m, h1, preferred_element_type=F32))


def _exact_dot_r(x, m01):
    m = m01.astype(BF16)
    h1, h2, h3 = _split3(x)
    return (jnp.dot(h3, m, preferred_element_type=F32) + jnp.dot(h2, m, preferred_element_type=F32)
            + jnp.dot(h1, m, preferred_element_type=F32))


def _gdn_kernel(q_ref, k_ref, v_ref, z_ref, ba_ref, cwq_ref, cwk_ref, cwv_ref, alog_ref, dtb_ref, onorm_ref,
                o_ref,
                state_ref, stateb_ref, qh_ref, kh_ref, vh_ref, beta_s, gcol_s, grow_s, npw_s, vb_s, kbe_s,
                u_s, wq_s, kt_s, attn_s, gl_s, *, heads, hg, ts):
    grp = pl.program_id(1)
    seg = pl.program_id(2)
    nch = ts // CHUNK
    dk = LANES
    w = hg * dk

    @pl.when(seg == 0)
    def _():
        state_ref[...] = jnp.zeros_like(state_ref)
        stateb_ref[...] = jnp.zeros_like(stateb_ref)
        for halo in (qh_ref, kh_ref, vh_ref):
            halo[...] = jnp.zeros_like(halo)

    ba = ba_ref[0]
    beta_c = jax.nn.sigmoid(ba)
    g_c = -jnp.exp(alog_ref[...]) * jax.nn.softplus(ba + dtb_ref[...])
    sel_row = lax.broadcasted_iota(jnp.int32, (LANES, w), 0)
    sel_head = lax.broadcasted_iota(jnp.int32, (LANES, w), 1) // dk + grp * hg
    beta_s[...] = _exact_dot_r(beta_c, sel_row == sel_head)

    ri = lax.broadcasted_iota(jnp.int32, (CHUNK, CHUNK), 0)
    ci = lax.broadcasted_iota(jnp.int32, (CHUNK, CHUNK), 1)
    tri = ri >= ci
    strict = ri > ci
    eye = (ri == ci).astype(F32)
    eye_w = (lax.broadcasted_iota(jnp.int32, (CHUNK, w), 1) % dk
             == lax.broadcasted_iota(jnp.int32, (CHUNK, w), 0))
    ones = jnp.ones((CHUNK, CHUNK), F32)
    gc_c = jnp.concatenate([_exact_dot_l(tri, g_c[c * CHUNK:(c + 1) * CHUNK]) for c in range(nch)], axis=0)
    gcol_s[...] = _exact_dot_r(gc_c, sel_row == sel_head + heads)
    for c in range(nch):
        rows = slice(c * CHUNK, (c + 1) * CHUNK)
        grow_s[rows, :] = _exact_dot_l(ones, jnp.where(eye_w, gcol_s[rows, :], 0.0))

    def conv_window(raw_ref, halo_ref, c, r, cols):
        back = pl.multiple_of(jnp.maximum(r - SUBLANES, 0), SUBLANES)
        prev = jnp.where(c == 0, halo_ref[:, cols], raw_ref[0, pl.ds(back, SUBLANES), cols])
        return jnp.concatenate([prev, raw_ref[0, pl.ds(r, CHUNK), cols]], axis=0)

    def conv_silu(win, cw_ref, cols):
        width = cw_ref.shape[0]
        acc = None
        for j in range(width):
            back = width - 1 - j
            shifted = pltpu.roll(win, back, axis=0) if back else win
            term = shifted[SUBLANES:] * cw_ref[j:j + 1, cols]
            acc = term if acc is None else acc + term
        return _silu(acc)

    def l2n(t):
        return t * lax.rsqrt(jnp.sum(t * t, axis=-1, keepdims=True) + EPS)

    head_cols = [slice(hh * dk, (hh + 1) * dk) for hh in range(hg)]
    half_cols = [slice(hh * dk, hh * dk + CHUNK) for hh in range(hg)]

    def rows_of(c):
        return pl.ds(pl.multiple_of(c * CHUNK, CHUNK), CHUNK)

    def prepare(c, heads_todo):
        r = pl.multiple_of(c * CHUNK, CHUNK)
        r2 = pl.multiple_of(c * (2 * CHUNK), 2 * CHUNK)
        r8 = pl.multiple_of(c * SUBLANES, SUBLANES)
        for hh in heads_todo:
            cols, half = head_cols[hh], half_cols[hh]
            q = l2n(conv_silu(conv_window(q_ref, qh_ref, c, r, cols), cwq_ref, cols)) * (dk ** -0.5)
            k = l2n(conv_silu(conv_window(k_ref, kh_ref, c, r, cols), cwk_ref, cols))
            v = conv_silu(conv_window(v_ref, vh_ref, c, r, cols), cwv_ref, cols)
            beta = beta_s[pl.ds(r, CHUNK), cols]
            gcol = gcol_s[pl.ds(r, CHUNK), cols]
            decay = jnp.exp(jnp.where(tri, gcol[:, :CHUNK] - grow_s[pl.ds(r, CHUNK), half], -jnp.inf))
            kb = k * beta
            eg = jnp.exp(gcol)
            gl = gcol[CHUNK - 1:CHUNK, :]
            npw_s[pl.ds(r, CHUNK), half] = -jnp.where(strict, _dot_nt(kb, k) * decay, 0.0)
            vb_s[pl.ds(r, CHUNK), cols] = (v * beta).astype(BF16)
            kbe_s[pl.ds(r, CHUNK), cols] = (kb * eg).astype(BF16)
            wq_s[pl.ds(r2 + CHUNK, CHUNK), cols] = (q * eg).astype(BF16)
            kt_s[pl.ds(r, CHUNK), cols] = (k * jnp.exp(gl - gcol)).astype(BF16)
            attn_s[pl.ds(r, CHUNK), half] = (_dot_nt(q, k) * decay).astype(BF16)
            gl_s[pl.ds(r8, SUBLANES), cols] = jnp.broadcast_to(jnp.exp(gl), (SUBLANES, dk))

    def load_npw(c):
        return [npw_s[rows_of(c), half] for half in half_cols]

    def solve_stages(c, npw):
        r = pl.multiple_of(c * CHUNK, CHUNK)
        r2 = pl.multiple_of(c * (2 * CHUNK), 2 * CHUNK)
        t_inv = [eye + p for p in npw]
        pw = [_bdot(p, p) for p in npw]
        yield
        for _ in range(CHUNK.bit_length() - 3):
            pw_b = [p.astype(BF16) for p in pw]
            t_inv = [t + jnp.dot(p, t.astype(BF16), preferred_element_type=F32) for p, t in zip(pw_b, t_inv)]
            pw = [jnp.dot(p, p, preferred_element_type=F32) for p in pw_b]
            yield
        t_inv = [t + _bdot(p, t) for p, t in zip(pw, t_inv)]
        yield
        for cols, t in zip(head_cols, t_inv):
            rhs = jnp.concatenate([vb_s[pl.ds(r, CHUNK), cols], kbe_s[pl.ds(r, CHUNK), cols]], axis=1)
            sol = jnp.dot(t.astype(BF16), rhs, preferred_element_type=F32)
            u_s[pl.ds(r, CHUNK), cols] = sol[:, :dk]
            wq_s[pl.ds(r2, CHUNK), cols] = sol[:, dk:].astype(BF16)

    def recur_stages(c):
        r = pl.multiple_of(c * CHUNK, CHUNK)
        r2 = pl.multiple_of(c * (2 * CHUNK), 2 * CHUNK)
        r8 = pl.multiple_of(c * SUBLANES, SUBLANES)
        hs = range(hg)
        t = [jnp.dot(wq_s[pl.ds(r2, 2 * CHUNK), head_cols[h]], stateb_ref[h], preferred_element_type=F32)
             for h in hs]
        yield
        v_new = [(u_s[pl.ds(r, CHUNK), head_cols[h]] - t[h][:CHUNK]).astype(BF16) for h in hs]
        o = [t[h][CHUNK:] + jnp.dot(attn_s[pl.ds(r, CHUNK), half_cols[h]], v_new[h], preferred_element_type=F32)
             for h in hs]
        upd = [_dot_tn(kt_s[pl.ds(r, CHUNK), head_cols[h]], v_new[h]) for h in hs]
        yield
        for h in hs:
            s_new = state_ref[h] * gl_s[pl.ds(r8, 1), head_cols[h]] + upd[h]
            state_ref[h] = s_new
            stateb_ref[h] = s_new.astype(BF16)
            gated = _rms(o[h], onorm_ref[...]) * _silu(z_ref[0, pl.ds(r, CHUNK), head_cols[h]])
            o_ref[0, pl.ds(r, CHUNK), head_cols[h]] = gated.astype(o_ref.dtype)

    def pipeline_body(prep_c=None, solve_c=None, recur_c=None):
        staged = []
        if solve_c is not None:
            staged.append(solve_stages(solve_c, load_npw(solve_c)))
        if recur_c is not None:
            staged.append(recur_stages(recur_c))
        for hh in range(hg):
            for stages in staged:
                next(stages, None)
            if prep_c is not None:
                prepare(prep_c, [hh])
        for stages in staged:
            for _ in stages:
                pass

    def steady(c, carry):
        pipeline_body(c, c - 1, c - 2)
        return carry

    pipeline_body(prep_c=0)
    if nch > 1:
        pipeline_body(prep_c=1, solve_c=0)
    lax.fori_loop(2, nch, steady, 0)
    pipeline_body(solve_c=nch - 1, recur_c=nch - 2 if nch > 1 else None)
    pipeline_body(recur_c=nch - 1)
    for halo, raw in ((qh_ref, q_ref), (kh_ref, k_ref), (vh_ref, v_ref)):
        halo[...] = raw[0, ts - SUBLANES:ts, :]


def _gdn(p3, ba3, conv_w, a_log, dt_bias, o_norm, heads, hg, ts):
    bsz, t, _ = p3.shape
    dk = LANES
    w = hg * dk
    ngrp = heads // hg
    nch = ts // CHUNK
    col = lambda part: pl.BlockSpec((1, ts, w), lambda b, g, s: (b, s, part * ngrp + g))
    cw = lambda part: pl.BlockSpec((conv_w.shape[0], w), lambda b, g, s: (0, part * ngrp + g))
    vec = pl.BlockSpec((1, LANES), lambda b, g, s: (0, 0))
    gate_lanes = lambda a: jnp.pad(a, (heads, LANES - 2 * heads)).reshape(1, LANES)
    return pl.pallas_call(
        functools.partial(_gdn_kernel, heads=heads, hg=hg, ts=ts),
        grid=(bsz, ngrp, t // ts),
        in_specs=[col(0), col(1), col(2), col(3),
                  pl.BlockSpec((1, ts, LANES), lambda b, g, s: (b, s, 0)),
                  cw(0), cw(1), cw(2), vec, vec, vec],
        out_specs=pl.BlockSpec((1, ts, w), lambda b, g, s: (b, s, g)),
        out_shape=jax.ShapeDtypeStruct((bsz, t, heads * dk), BF16),
        scratch_shapes=[
            pltpu.VMEM((hg, dk, dk), F32), pltpu.VMEM((hg, dk, dk), BF16),
            pltpu.VMEM((SUBLANES, w), F32), pltpu.VMEM((SUBLANES, w), F32), pltpu.VMEM((SUBLANES, w), F32),
            pltpu.VMEM((ts, w), F32),
            pltpu.VMEM((ts, w), F32), pltpu.VMEM((ts, w), F32),
            pltpu.VMEM((ts, w), F32),
            pltpu.VMEM((ts, w), BF16), pltpu.VMEM((ts, w), BF16),
            pltpu.VMEM((ts, w), F32),
            pltpu.VMEM((2 * ts, w), BF16),
            pltpu.VMEM((ts, w), BF16),
            pltpu.VMEM((ts, w), BF16),
            pltpu.VMEM((SUBLANES * nch, w), F32),
        ],
        compiler_params=_cparams(3),
        name="gdn",
    )(p3, p3, p3, p3, ba3, conv_w, conv_w, conv_w, gate_lanes(a_log), gate_lanes(dt_bias),
      o_norm.reshape(1, dk))


def _gmlp_kernel(u_ref, v_ref, lng_ref, lnb_ref, ws_ref, bs_ref, o_ref, *, groups, blk):
    tb = u_ref.shape[0]
    pos_i = lax.broadcasted_iota(jnp.int32, (blk, blk), 0) // CHUNK
    pos_j = lax.broadcasted_iota(jnp.int32, (blk, blk), 1) // CHUNK
    mask = pos_j <= pos_i
    for g in range(groups):
        cs = slice(g * LANES, (g + 1) * LANES)
        vg = _gelu(v_ref[:, cs])
        mu = jnp.mean(vg, axis=-1, keepdims=True)
        vc = vg - mu
        vn = vc * lax.rsqrt(jnp.mean(vc * vc, axis=-1, keepdims=True) + EPS)
        vn = (vn * lng_ref[:, cs] + lnb_ref[:, cs]).astype(BF16)
        wm = jnp.where(mask, ws_ref[g], 0.0).astype(BF16)
        for m in range(tb // blk):
            rs = slice(m * blk, (m + 1) * blk)
            mixed = jnp.dot(wm, vn[rs], preferred_element_type=F32) + bs_ref[g]
            u = _gelu(u_ref[rs, cs])
            o_ref[rs, cs] = (u * mixed).astype(o_ref.dtype)


def _gmlp(p2, u_blk, v_blk, ln_g, ln_b, w_s, bs_b, tb):
    m = p2.shape[0]
    groups, blk, _ = w_s.shape
    w = groups * LANES
    return pl.pallas_call(
        functools.partial(_gmlp_kernel, groups=groups, blk=blk),
        grid=(m // tb,),
        in_specs=[
            pl.BlockSpec((tb, w), lambda i: (i, u_blk)),
            pl.BlockSpec((tb, w), lambda i: (i, v_blk)),
            pl.BlockSpec((1, w), lambda i: (0, 0)),
            pl.BlockSpec((1, w), lambda i: (0, 0)),
            pl.BlockSpec((groups, blk, blk), lambda i: (0, 0, 0)),
            pl.BlockSpec((groups, blk, LANES), lambda i: (0, 0, 0)),
        ],
        out_specs=pl.BlockSpec((tb, w), lambda i: (i, 0)),
        out_shape=jax.ShapeDtypeStruct((m, w), BF16),
        compiler_params=_cparams(1),
        name="gmlp",
    )(p2, p2, ln_g.reshape(1, w), ln_b.reshape(1, w), w_s, bs_b)


def _outproj_kernel(x_ref, a_ref, b_ref, wa_ref, wb_ref, o_ref):
    o_ref[...] = (x_ref[...]
                  + jnp.dot(a_ref[...], wa_ref[...], preferred_element_type=F32)
                  + jnp.dot(b_ref[...], wb_ref[...], preferred_element_type=F32))


def _outproj(x2, oa, ob, w_out, bm, bn):
    m, d = x2.shape
    ka = oa.shape[1]
    kb = ob.shape[1]
    return pl.pallas_call(
        _outproj_kernel,
        grid=(m // bm, d // bn),
        in_specs=[
            pl.BlockSpec((bm, bn), lambda i, j: (i, j)),
            pl.BlockSpec((bm, ka), lambda i, j: (i, 0)),
            pl.BlockSpec((bm, kb), lambda i, j: (i, 0)),
            pl.BlockSpec((ka, bn), lambda i, j: (0, j)),
            pl.BlockSpec((kb, bn), lambda i, j: (ka // kb, j)),
        ],
        out_specs=pl.BlockSpec((bm, bn), lambda i, j: (i, j)),
        out_shape=jax.ShapeDtypeStruct((m, d), F32),
        compiler_params=_cparams(2),
        name="outproj",
    )(x2, oa, ob, w_out, w_out)


def _mlp_kernel(x_ref, g_ref, w1_ref, w2_ref, fg_ref, o_ref, xn_ref, *, final_norm):
    k = pl.program_id(1)

    @pl.when(k == 0)
    def _():
        x = x_ref[...]
        xn_ref[...] = _rms(x, g_ref[...]).astype(BF16)
        o_ref[...] = x

    a = jnp.maximum(jnp.dot(xn_ref[...], w1_ref[0], preferred_element_type=F32), 0.0)
    o_ref[...] += jnp.dot((a * a).astype(BF16), w2_ref[0], preferred_element_type=F32)

    if final_norm:
        @pl.when(k == pl.num_programs(1) - 1)
        def _():
            o_ref[...] = _rms(o_ref[...], fg_ref[...])


def _mlp(x2, gain, w1, w2, layer, final_gain, bm, bf, final_norm):
    m, d = x2.shape
    f = w1.shape[2]
    return pl.pallas_call(
        functools.partial(_mlp_kernel, final_norm=final_norm),
        grid=(m // bm, f // bf),
        in_specs=[
            pl.BlockSpec((bm, d), lambda i, k: (i, 0)),
            pl.BlockSpec((1, d), lambda i, k: (0, 0)),
            pl.BlockSpec((1, d, bf), lambda i, k: (layer, 0, k)),
            pl.BlockSpec((1, bf, d), lambda i, k: (layer, k, 0)),
            pl.BlockSpec((1, d), lambda i, k: (0, 0)),
        ],
        out_specs=pl.BlockSpec((bm, d), lambda i, k: (i, 0)),
        out_shape=jax.ShapeDtypeStruct((m, d), F32),
        scratch_shapes=[pltpu.VMEM((bm, d), BF16)],
        compiler_params=_cparams(2),
        name="mlp_final" if final_norm else "mlp",
    )(x2, gain.reshape(1, d), w1, w2, final_gain.reshape(1, d))


def _gluconv_kernel(x_ref, g_ref, w_ref, b_ref, dw_ref, dwb_ref, lng_ref, lnb_ref, o_ref,
                    xn_ref, cur, prev, zc_ref, *, width, tiles_per_seq, nchunk):
    s = pl.program_id(0)
    bm, d = x_ref.shape
    rb = zc_ref.shape[0]

    @pl.when(s == 0)
    def _():
        cur[...] = jnp.zeros_like(cur)
        prev[...] = jnp.zeros_like(prev)

    starts_sequence = ((s - 1) % tiles_per_seq) == 0
    prev[0:HALO, :] = jnp.where(starts_sequence, 0.0, prev[bm:bm + HALO, :])
    prev[HALO:HALO + bm, :] = cur[...]
    xn_ref[...] = _rms(x_ref[...], g_ref[...]).astype(BF16)

    base = HALO - (width - 1)

    def conv_block(r):
        for c0 in range(0, d, LANES):
            cols = slice(c0, c0 + LANES)
            out = jnp.broadcast_to(dwb_ref[:, cols], (rb, LANES))
            for ph in range(SUBLANES):
                part = None
                for j in range(width):
                    if (base + j) % SUBLANES != ph:
                        continue
                    lo = r + base + j - ph
                    win = prev[lo:lo + rb + (SUBLANES if ph else 0), cols]
                    term = win * dw_ref[j:j + 1, cols]
                    part = term if part is None else part + term
                if part is not None:
                    if ph:
                        part = pltpu.roll(part, rb + SUBLANES - ph, axis=0)
                    out = out + part[:rb]
            zc_ref[:, cols] = out
        for r0 in range(0, rb, BF16_ROWS):
            acc = zc_ref[r0:r0 + BF16_ROWS, :]
            mu = jnp.mean(acc, axis=-1, keepdims=True)
            xc = acc - mu
            y = xc * lax.rsqrt(jnp.mean(xc * xc, axis=-1, keepdims=True) + EPS)
            y = _silu(y * lng_ref[...] + lnb_ref[...])
            o_ref[r + r0:r + r0 + BF16_ROWS, :] = y.astype(o_ref.dtype)

    cw = d // nchunk
    blocks_per_chunk = (bm // rb) // nchunk
    for c in range(nchunk):
        lin = slice(c * cw, (c + 1) * cw)
        gate = slice(d + c * cw, d + (c + 1) * cw)
        a = jnp.dot(xn_ref[...], w_ref[:, lin], preferred_element_type=F32) + b_ref[:, lin]
        b = jnp.dot(xn_ref[...], w_ref[:, gate], preferred_element_type=F32) + b_ref[:, gate]
        cur[:, lin] = a * jax.nn.sigmoid(b)
        for blk in range(c * blocks_per_chunk, (c + 1) * blocks_per_chunk):
            conv_block(blk * rb)


def _gluconv(x2, gain, pw1, pw1_b, dw, dw_b, ln_g, ln_b, seq, bm, nchunk, rb):
    m, d = x2.shape
    width = dw.shape[0]
    n_tiles = m // bm
    row = lambda a: a.reshape(1, -1)
    const = lambda shape: pl.BlockSpec(shape, lambda s: (0, 0))
    return pl.pallas_call(
        functools.partial(_gluconv_kernel, width=width, tiles_per_seq=seq // bm, nchunk=nchunk),
        grid=(n_tiles + 1,),
        in_specs=[
            pl.BlockSpec((bm, d), lambda s: (jnp.minimum(s, n_tiles - 1), 0)),
            const((1, d)),
            pl.BlockSpec((d, 2 * d), lambda s: (0, 0), pipeline_mode=pl.Buffered(1)),
            const((1, 2 * d)), const((width, d)), const((1, d)), const((1, d)), const((1, d)),
        ],
        out_specs=pl.BlockSpec((bm, d), lambda s: (jnp.maximum(s - 1, 0), 0)),
        out_shape=jax.ShapeDtypeStruct((m, d), BF16),
        scratch_shapes=[pltpu.VMEM((bm, d), BF16), pltpu.VMEM((bm, d), F32), pltpu.VMEM((HALO + bm, d), F32),
                        pltpu.VMEM((rb, d), F32)],
        compiler_params=_cparams(1),
        name="gluconv",
    )(x2, row(gain), pw1, row(pw1_b), dw, row(dw_b), row(ln_g), row(ln_b))


def _pw2_kernel(x_ref, z_ref, w_ref, b_ref, o_ref):
    o_ref[...] = x_ref[...] + jnp.dot(z_ref[...], w_ref[...], preferred_element_type=F32) + b_ref[...]


def _pw2(x2, zs, pw2, pw2_b, bm):
    m, d = x2.shape
    return pl.pallas_call(
        _pw2_kernel,
        grid=(m // bm,),
        in_specs=[
            pl.BlockSpec((bm, d), lambda i: (i, 0)),
            pl.BlockSpec((bm, d), lambda i: (i, 0)),
            pl.BlockSpec((d, d), lambda i: (0, 0)),
            pl.BlockSpec((1, d), lambda i: (0, 0)),
        ],
        out_specs=pl.BlockSpec((bm, d), lambda i: (i, 0)),
        out_shape=jax.ShapeDtypeStruct((m, d), F32),
        compiler_params=_cparams(1),
        name="pw2",
    )(x2, zs, pw2, pw2_b.reshape(1, d))


def _tiles(rows, seq, d, d_ff, in_cols, heads):
    return dict(
        inproj=dict(bm=min(1024, rows), bn=min(1024, in_cols)),
        gdn=dict(hg=min(8, heads), ts=min(512, seq)),
        gmlp=dict(tb=min(512, rows)),
        outproj=dict(bm=min(512, rows), bn=d),
        mlp=dict(bm=min(512, rows), bf=min(2048, d_ff)),
        gluconv=dict(bm=min(512, seq), nchunk=max(1, d // MXU_COLS), rb=CHUNK),
        pw2=dict(bm=min(512, rows)),
    )


def _even_layer(x2, bsz, seq, tiles, norm_g, w_in, conv_w, a_log, dt_bias, o_norm, ln_g, ln_b, w_s, b_s, w_out):
    heads = a_log.shape[0]
    groups = w_s.shape[0]
    qkvz = 4 * heads * LANES
    gate0 = qkvz
    uv0 = gate0 + 2 * heads
    w_bf = w_in.astype(BF16)
    w_ba = jnp.pad(w_bf[:, gate0:uv0], ((0, 0), (0, LANES - 2 * heads)))
    p2, ba2 = _inproj(x2, norm_g, w_bf, qkvz, w_bf[:, uv0:], w_ba, **tiles["inproj"])
    p3 = p2.reshape(bsz, seq, -1)
    out_a = _gdn(p3, ba2.reshape(bsz, seq, LANES), conv_w, a_log, dt_bias, o_norm, heads, **tiles["gdn"])
    gw = groups * LANES
    bs_b = jnp.broadcast_to(b_s[:, :, None], b_s.shape + (LANES,))
    out_b = _gmlp(p2, qkvz // gw, qkvz // gw + 1, ln_g, ln_b, w_s, bs_b, **tiles["gmlp"])
    return _outproj(x2, out_a.reshape(bsz * seq, -1), out_b, w_out.astype(BF16), **tiles["outproj"])


def _odd_layer(x2, seq, tiles, norm_g, pw1, pw1_b, dw, dw_b, ln_g, ln_b, pw2, pw2_b):
    zs = _gluconv(x2, norm_g, pw1.astype(BF16), pw1_b, dw, dw_b, ln_g, ln_b, seq, **tiles["gluconv"])
    return _pw2(x2, zs, pw2.astype(BF16), pw2_b, **tiles["pw2"])


def kernel(x, e_norm, e_w_in, e_conv_w, e_a_log, e_dt_bias, e_o_norm, e_ln_g, e_ln_b, e_w_s, e_b_s, e_w_out, o_norm, o_pw1, o_pw1_b, o_dw, o_dw_b, o_ln_g, o_ln_b, o_pw2, o_pw2_b, f_norm, f_w1, f_w2, final_norm):
    bsz, seq, d = x.shape
    depth = f_norm.shape[0]
    x2 = x.reshape(bsz * seq, d)
    heads = e_a_log.shape[1]
    tiles = _tiles(bsz * seq, seq, d, f_w1.shape[2], e_w_in.shape[2] - 2 * heads, heads)
    w1_bf = f_w1.astype(BF16)
    w2_bf = f_w2.astype(BF16)
    for layer in range(depth):
        i = layer // 2
        if layer % 2 == 0:
            x2 = _even_layer(x2, bsz, seq, tiles, e_norm[i], e_w_in[i], e_conv_w[i], e_a_log[i], e_dt_bias[i],
                             e_o_norm[i], e_ln_g[i], e_ln_b[i], e_w_s[i], e_b_s[i], e_w_out[i])
        else:
            x2 = _odd_layer(x2, seq, tiles, o_norm[i], o_pw1[i], o_pw1_b[i], o_dw[i], o_dw_b[i], o_ln_g[i],
                            o_ln_b[i], o_pw2[i], o_pw2_b[i])
        x2 = _mlp(x2, f_norm[layer], w1_bf, w2_bf, layer, final_norm,
                  final_norm=(layer == depth - 1), **tiles["mlp"])
    return x2.reshape(bsz, seq, d)
```

```python
import functools

import jax
import jax.numpy as jnp
from jax import lax
from jax.experimental import pallas as pl
from jax.experimental.pallas import tpu as pltpu

F32 = jnp.float32
BF16 = jnp.bfloat16
EPS = 1e-6
CHUNK = 64
LANES = 128
SUBLANES = 8
BF16_ROWS = 16
MXU_COLS = 256
HALO = 32
VMEM_LIMIT = 56 * 1024 * 1024


def _cparams(n_axes):
    return pltpu.CompilerParams(dimension_semantics=("arbitrary",) * n_axes,
                                vmem_limit_bytes=VMEM_LIMIT)


def _bdot(a, b):
    return jnp.dot(a.astype(BF16), b.astype(BF16), preferred_element_type=F32)


def _dot_nt(a, b):
    return lax.dot_general(a.astype(BF16), b.astype(BF16), (((1,), (1,)), ((), ())),
                           preferred_element_type=F32)


def _dot_tn(a, b):
    return lax.dot_general(a.astype(BF16), b.astype(BF16), (((0,), (0,)), ((), ())),
                           preferred_element_type=F32)


def _rms(x, g):
    return x * lax.rsqrt(jnp.mean(x * x, axis=-1, keepdims=True) + EPS) * g


def _silu(x):
    return x * jax.nn.sigmoid(x)


def _gelu(x):
    return 0.5 * x * (1.0 + lax.erf(x * (2.0 ** -0.5)))


def _inproj_kernel(x_ref, g_ref, wa_ref, wb_ref, wba_ref, p_ref, ba_ref, xn_ref, *, na):
    j = pl.program_id(1)

    @pl.when(j == 0)
    def _():
        xn = _rms(x_ref[...], g_ref[...]).astype(BF16)
        xn_ref[...] = xn
        ba_ref[...] = jnp.dot(xn, wba_ref[...], preferred_element_type=F32)

    @pl.when(j < na)
    def _():
        p_ref[...] = jnp.dot(xn_ref[...], wa_ref[...], preferred_element_type=F32)

    @pl.when(j >= na)
    def _():
        p_ref[...] = jnp.dot(xn_ref[...], wb_ref[...], preferred_element_type=F32)


def _inproj(x2, gain, w_a, n_a, w_b, w_ba, bm, bn):
    m, d = x2.shape
    na = n_a // bn
    nb = w_b.shape[1] // bn
    n = n_a + w_b.shape[1]
    return pl.pallas_call(
        functools.partial(_inproj_kernel, na=na),
        grid=(m // bm, na + nb),
        in_specs=[
            pl.BlockSpec((bm, d), lambda i, j: (i, 0)),
            pl.BlockSpec((1, d), lambda i, j: (0, 0)),
            pl.BlockSpec((d, bn), lambda i, j: (0, jnp.minimum(j, na - 1))),
            pl.BlockSpec((d, bn), lambda i, j: (0, jnp.maximum(j - na, 0))),
            pl.BlockSpec((d, LANES), lambda i, j: (0, 0)),
        ],
        out_specs=[
            pl.BlockSpec((bm, bn), lambda i, j: (i, j)),
            pl.BlockSpec((bm, LANES), lambda i, j: (i, 0)),
        ],
        out_shape=[jax.ShapeDtypeStruct((m, n), F32), jax.ShapeDtypeStruct((m, LANES), F32)],
        scratch_shapes=[pltpu.VMEM((bm, d), BF16)],
        compiler_params=_cparams(2),
        name="inproj",
    )(x2, gain.reshape(1, d), w_a, w_b, w_ba)


def _split3(x):
    h1 = x.astype(BF16)
    r1 = x - h1.astype(F32)
    h2 = r1.astype(BF16)
    h3 = (r1 - h2.astype(F32)).astype(BF16)
    return h1, h2, h3


def _exact_dot_l(m01, x):
    m = m01.astype(BF16)
    h1, h2, h3 = _split3(x)
    return (jnp.dot(m, h3, preferred_element_type=F32) + jnp.dot(m, h2, preferred_element_type=F32)
            + jnp.dot(m, h1, preferred_element_type=F32))


def _gdn_kernel(q_ref, k_ref, v_ref, z_ref, ba_ref, cwq_ref, cwk_ref, cwv_ref, alog_ref, dtb_ref, onorm_ref,
                o_ref,
                state_ref, stateb_ref, qh_ref, kh_ref, vh_ref, betac_s, gcc_s, gct_s, npw_s, vb_s, kbe_s,
                u_s, wq_s, kt_s, attn_s, gl_s, *, heads, hg, ts):
    seg = pl.program_id(2)
    nch = ts // CHUNK
    dk = LANES
    w = hg * dk

    @pl.when(seg == 0)
    def _():
        state_ref[...] = jnp.zeros_like(state_ref)
        stateb_ref[...] = jnp.zeros_like(stateb_ref)
        for halo in (qh_ref, kh_ref, vh_ref):
            halo[...] = jnp.zeros_like(halo)

    ba = ba_ref[0]
    betac_s[...] = jax.nn.sigmoid(ba)
    g_c = -jnp.exp(alog_ref[...]) * jax.nn.softplus(ba + dtb_ref[...])

    ri = lax.broadcasted_iota(jnp.int32, (CHUNK, CHUNK), 0)
    ci = lax.broadcasted_iota(jnp.int32, (CHUNK, CHUNK), 1)
    tri = ri >= ci
    strict = ri > ci
    eye = (ri == ci).astype(F32)
    for c in range(nch):
        gc = _exact_dot_l(tri, g_c[c * CHUNK:(c + 1) * CHUNK])
        gcc_s[c * CHUNK:(c + 1) * CHUNK, :] = gc
        gct_s[c * LANES:(c + 1) * LANES, :] = jnp.concatenate([gc, jnp.zeros_like(gc)], axis=0).T

    def conv_window(raw_ref, halo_ref, c, r, cols):
        back = pl.multiple_of(jnp.maximum(r - SUBLANES, 0), SUBLANES)
        prev = jnp.where(c == 0, halo_ref[:, cols], raw_ref[0, pl.ds(back, SUBLANES), cols])
        return jnp.concatenate([prev, raw_ref[0, pl.ds(r, CHUNK), cols]], axis=0)

    def conv_silu(win, cw_ref, cols):
        width = cw_ref.shape[0]
        acc = None
        for j in range(width):
            back = width - 1 - j
            shifted = pltpu.roll(win, back, axis=0) if back else win
            term = shifted[SUBLANES:] * cw_ref[j:j + 1, cols]
            acc = term if acc is None else acc + term
        return _silu(acc)

    def l2n(t):
        return t * lax.rsqrt(jnp.sum(t * t, axis=-1, keepdims=True) + EPS)

    head_cols = [slice(hh * dk, (hh + 1) * dk) for hh in range(hg)]
    half_cols = [slice(hh * dk, hh * dk + CHUNK) for hh in range(hg)]

    def rows_of(c):
        return pl.ds(pl.multiple_of(c * CHUNK, CHUNK), CHUNK)

    def prepare(c, heads_todo):
        r = pl.multiple_of(c * CHUNK, CHUNK)
        r2 = pl.multiple_of(c * (2 * CHUNK), 2 * CHUNK)
        r8 = pl.multiple_of(c * SUBLANES, SUBLANES)
        beta_c = betac_s[pl.ds(r, CHUNK), :]
        gc_c = gcc_s[pl.ds(r, CHUNK), :]
        rt = pl.multiple_of(c * LANES, LANES)
        for hh in heads_todo:
            cols, half = head_cols[hh], half_cols[hh]
            q = l2n(conv_silu(conv_window(q_ref, qh_ref, c, r, cols), cwq_ref, cols)) * (dk ** -0.5)
            k = l2n(conv_silu(conv_window(k_ref, kh_ref, c, r, cols), cwk_ref, cols))
            v = conv_silu(conv_window(v_ref, vh_ref, c, r, cols), cwv_ref, cols)
            lane = heads + hh
            beta = jnp.broadcast_to(beta_c[:, hh:hh + 1], (CHUNK, dk))
            gcol = jnp.broadcast_to(gc_c[:, lane:lane + 1], (CHUNK, dk))
            gct = gct_s[pl.ds(rt + lane // SUBLANES * SUBLANES, SUBLANES), 0:CHUNK]
            grow = jnp.broadcast_to(gct[lane % SUBLANES:lane % SUBLANES + 1, :], (CHUNK, CHUNK))
            decay = jnp.exp(jnp.where(tri, gcol[:, :CHUNK] - grow, -jnp.inf))
            kb = k * beta
            eg = jnp.exp(gcol)
            gl = gcol[CHUNK - 1:CHUNK, :]
            npw_s[pl.ds(r, CHUNK), half] = -jnp.where(strict, _dot_nt(kb, k) * decay, 0.0)
            vb_s[pl.ds(r, CHUNK), cols] = (v * beta).astype(BF16)
            kbe_s[pl.ds(r, CHUNK), cols] = (kb * eg).astype(BF16)
            wq_s[pl.ds(r2 + CHUNK, CHUNK), cols] = (q * eg).astype(BF16)
            kt_s[pl.ds(r, CHUNK), cols] = (k * jnp.exp(gl - gcol)).astype(BF16)
            attn_s[pl.ds(r, CHUNK), half] = (_dot_nt(q, k) * decay).astype(BF16)
            gl_s[pl.ds(r8, SUBLANES), cols] = jnp.broadcast_to(jnp.exp(gl), (SUBLANES, dk))

    def load_npw(c):
        return [npw_s[rows_of(c), half] for half in half_cols]

    def solve_stages(c, npw):
        r = pl.multiple_of(c * CHUNK, CHUNK)
        r2 = pl.multiple_of(c * (2 * CHUNK), 2 * CHUNK)
        t_inv = [eye + p for p in npw]
        pw = [_bdot(p, p) for p in npw]
        yield
        for _ in range(CHUNK.bit_length() - 3):
            pw_b = [p.astype(BF16) for p in pw]
            t_inv = [t + jnp.dot(p, t.astype(BF16), preferred_element_type=F32) for p, t in zip(pw_b, t_inv)]
            pw = [jnp.dot(p, p, preferred_element_type=F32) for p in pw_b]
            yield
        t_inv = [t + _bdot(p, t) for p, t in zip(pw, t_inv)]
        yield
        for cols, t in zip(head_cols, t_inv):
            rhs = jnp.concatenate([vb_s[pl.ds(r, CHUNK), cols], kbe_s[pl.ds(r, CHUNK), cols]], axis=1)
            sol = jnp.dot(t.astype(BF16), rhs, preferred_element_type=F32)
            u_s[pl.ds(r, CHUNK), cols] = sol[:, :dk]
            wq_s[pl.ds(r2, CHUNK), cols] = sol[:, dk:].astype(BF16)

    def recur_stages(c):
        r = pl.multiple_of(c * CHUNK, CHUNK)
        r2 = pl.multiple_of(c * (2 * CHUNK), 2 * CHUNK)
        r8 = pl.multiple_of(c * SUBLANES, SUBLANES)
        hs = range(hg)
        t = [jnp.dot(wq_s[pl.ds(r2, 2 * CHUNK), head_cols[h]], stateb_ref[h], preferred_element_type=F32)
             for h in hs]
        yield
        v_new = [(u_s[pl.ds(r, CHUNK), head_cols[h]] - t[h][:CHUNK]).astype(BF16) for h in hs]
        o = [t[h][CHUNK:] + jnp.dot(attn_s[pl.ds(r, CHUNK), half_cols[h]], v_new[h], preferred_element_type=F32)
             for h in hs]
        upd = [_dot_tn(kt_s[pl.ds(r, CHUNK), head_cols[h]], v_new[h]) for h in hs]
        yield
        for h in hs:
            s_new = state_ref[h] * gl_s[pl.ds(r8, 1), head_cols[h]] + upd[h]
            state_ref[h] = s_new
            stateb_ref[h] = s_new.astype(BF16)
            gated = _rms(o[h], onorm_ref[...]) * _silu(z_ref[0, pl.ds(r, CHUNK), head_cols[h]])
            o_ref[0, pl.ds(r, CHUNK), head_cols[h]] = gated.astype(o_ref.dtype)

    def pipeline_body(prep_c=None, solve_c=None, recur_c=None):
        staged = []
        if solve_c is not None:
            staged.append(solve_stages(solve_c, load_npw(solve_c)))
        if recur_c is not None:
            staged.append(recur_stages(recur_c))
        for hh in range(hg):
            for stages in staged:
                next(stages, None)
            if prep_c is not None:
                prepare(prep_c, [hh])
        for stages in staged:
            for _ in stages:
                pass

    def steady(c, carry):
        pipeline_body(c, c - 1, c - 2)
        return carry

    pipeline_body(prep_c=0)
    if nch > 1:
        pipeline_body(prep_c=1, solve_c=0)
    lax.fori_loop(2, nch, steady, 0)
    pipeline_body(solve_c=nch - 1, recur_c=nch - 2 if nch > 1 else None)
    pipeline_body(recur_c=nch - 1)
    for halo, raw in ((qh_ref, q_ref), (kh_ref, k_ref), (vh_ref, v_ref)):
        halo[...] = raw[0, ts - SUBLANES:ts, :]


def _gdn(p3, ba3, conv_w, a_log, dt_bias, o_norm, heads, hg, ts):
    bsz, t, _ = p3.shape
    assert hg == heads, "the kernel picks each head's gate lane statically: all heads in one group"
    dk = LANES
    w = hg * dk
    ngrp = heads // hg
    nch = ts // CHUNK
    col = lambda part: pl.BlockSpec((1, ts, w), lambda b, g, s: (b, s, part * ngrp + g))
    cw = lambda part: pl.BlockSpec((conv_w.shape[0], w), lambda b, g, s: (0, part * ngrp + g))
    vec = pl.BlockSpec((1, LANES), lambda b, g, s: (0, 0))
    gate_lanes = lambda a: jnp.pad(a, (heads, LANES - 2 * heads)).reshape(1, LANES)
    return pl.pallas_call(
        functools.partial(_gdn_kernel, heads=heads, hg=hg, ts=ts),
        grid=(bsz, ngrp, t // ts),
        in_specs=[col(0), col(1), col(2), col(3),
                  pl.BlockSpec((1, ts, LANES), lambda b, g, s: (b, s, 0)),
                  cw(0), cw(1), cw(2), vec, vec, vec],
        out_specs=pl.BlockSpec((1, ts, w), lambda b, g, s: (b, s, g)),
        out_shape=jax.ShapeDtypeStruct((bsz, t, heads * dk), BF16),
        scratch_shapes=[
            pltpu.VMEM((hg, dk, dk), F32), pltpu.VMEM((hg, dk, dk), BF16),
            pltpu.VMEM((SUBLANES, w), F32), pltpu.VMEM((SUBLANES, w), F32), pltpu.VMEM((SUBLANES, w), F32),
            pltpu.VMEM((ts, LANES), F32), pltpu.VMEM((ts, LANES), F32),
            pltpu.VMEM((nch * LANES, LANES), F32),
            pltpu.VMEM((ts, w), F32),
            pltpu.VMEM((ts, w), BF16), pltpu.VMEM((ts, w), BF16),
            pltpu.VMEM((ts, w), F32),
            pltpu.VMEM((2 * ts, w), BF16),
            pltpu.VMEM((ts, w), BF16),
            pltpu.VMEM((ts, w), BF16),
            pltpu.VMEM((SUBLANES * nch, w), F32),
        ],
        compiler_params=_cparams(3),
        name="gdn",
    )(p3, p3, p3, p3, ba3, conv_w, conv_w, conv_w, gate_lanes(a_log), gate_lanes(dt_bias),
      o_norm.reshape(1, dk))


def _gmlp_kernel(u_ref, v_ref, lng_ref, lnb_ref, ws_ref, bs_ref, o_ref, *, groups, blk):
    tb = u_ref.shape[0]
    pos_i = lax.broadcasted_iota(jnp.int32, (blk, blk), 0) // CHUNK
    pos_j = lax.broadcasted_iota(jnp.int32, (blk, blk), 1) // CHUNK
    mask = pos_j <= pos_i
    for g in range(groups):
        cs = slice(g * LANES, (g + 1) * LANES)
        vg = _gelu(v_ref[:, cs])
        mu = jnp.mean(vg, axis=-1, keepdims=True)
        vc = vg - mu
        vn = vc * lax.rsqrt(jnp.mean(vc * vc, axis=-1, keepdims=True) + EPS)
        vn = (vn * lng_ref[:, cs] + lnb_ref[:, cs]).astype(BF16)
        wm = jnp.where(mask, ws_ref[g], 0.0).astype(BF16)
        for m in range(tb // blk):
            rs = slice(m * blk, (m + 1) * blk)
            mixed = jnp.dot(wm, vn[rs], preferred_element_type=F32) + bs_ref[g]
            u = _gelu(u_ref[rs, cs])
            o_ref[rs, cs] = (u * mixed).astype(o_ref.dtype)


def _gmlp(p2, u_blk, v_blk, ln_g, ln_b, w_s, bs_b, tb):
    m = p2.shape[0]
    groups, blk, _ = w_s.shape
    w = groups * LANES
    return pl.pallas_call(
        functools.partial(_gmlp_kernel, groups=groups, blk=blk),
        grid=(m // tb,),
        in_specs=[
            pl.BlockSpec((tb, w), lambda i: (i, u_blk)),
            pl.BlockSpec((tb, w), lambda i: (i, v_blk)),
            pl.BlockSpec((1, w), lambda i: (0, 0)),
            pl.BlockSpec((1, w), lambda i: (0, 0)),
            pl.BlockSpec((groups, blk, blk), lambda i: (0, 0, 0)),
            pl.BlockSpec((groups, blk, LANES), lambda i: (0, 0, 0)),
        ],
        out_specs=pl.BlockSpec((tb, w), lambda i: (i, 0)),
        out_shape=jax.ShapeDtypeStruct((m, w), BF16),
        compiler_params=_cparams(1),
        name="gmlp",
    )(p2, p2, ln_g.reshape(1, w), ln_b.reshape(1, w), w_s, bs_b)


def _outproj_kernel(x_ref, a_ref, b_ref, wa_ref, wb_ref, o_ref):
    o_ref[...] = (x_ref[...]
                  + jnp.dot(a_ref[...], wa_ref[...], preferred_element_type=F32)
                  + jnp.dot(b_ref[...], wb_ref[...], preferred_element_type=F32))


def _outproj(x2, oa, ob, w_out, bm, bn):
    m, d = x2.shape
    ka = oa.shape[1]
    kb = ob.shape[1]
    return pl.pallas_call(
        _outproj_kernel,
        grid=(m // bm, d // bn),
        in_specs=[
            pl.BlockSpec((bm, bn), lambda i, j: (i, j)),
            pl.BlockSpec((bm, ka), lambda i, j: (i, 0)),
            pl.BlockSpec((bm, kb), lambda i, j: (i, 0)),
            pl.BlockSpec((ka, bn), lambda i, j: (0, j)),
            pl.BlockSpec((kb, bn), lambda i, j: (ka // kb, j)),
        ],
        out_specs=pl.BlockSpec((bm, bn), lambda i, j: (i, j)),
        out_shape=jax.ShapeDtypeStruct((m, d), F32),
        compiler_params=_cparams(2),
        name="outproj",
    )(x2, oa, ob, w_out, w_out)


def _mlp_kernel(x_ref, g_ref, w1_ref, w2_ref, fg_ref, o_ref, xn_ref, *, final_norm):
    k = pl.program_id(1)

    @pl.when(k == 0)
    def _():
        x = x_ref[...]
        xn_ref[...] = _rms(x, g_ref[...]).astype(BF16)
        o_ref[...] = x

    a = jnp.maximum(jnp.dot(xn_ref[...], w1_ref[0], preferred_element_type=F32), 0.0)
    o_ref[...] += jnp.dot((a * a).astype(BF16), w2_ref[0], preferred_element_type=F32)

    if final_norm:
        @pl.when(k == pl.num_programs(1) - 1)
        def _():
            o_ref[...] = _rms(o_ref[...], fg_ref[...])


def _mlp(x2, gain, w1, w2, layer, final_gain, bm, bf, final_norm):
    m, d = x2.shape
    f = w1.shape[2]
    return pl.pallas_call(
        functools.partial(_mlp_kernel, final_norm=final_norm),
        grid=(m // bm, f // bf),
        in_specs=[
            pl.BlockSpec((bm, d), lambda i, k: (i, 0)),
            pl.BlockSpec((1, d), lambda i, k: (0, 0)),
            pl.BlockSpec((1, d, bf), lambda i, k: (layer, 0, k)),
            pl.BlockSpec((1, bf, d), lambda i, k: (layer, k, 0)),
            pl.BlockSpec((1, d), lambda i, k: (0, 0)),
        ],
        out_specs=pl.BlockSpec((bm, d), lambda i, k: (i, 0)),
        out_shape=jax.ShapeDtypeStruct((m, d), F32),
        scratch_shapes=[pltpu.VMEM((bm, d), BF16)],
        compiler_params=_cparams(2),
        name="mlp_final" if final_norm else "mlp",
    )(x2, gain.reshape(1, d), w1, w2, final_gain.reshape(1, d))


def _gluconv_kernel(x_ref, g_ref, w_ref, b_ref, dw_ref, dwb_ref, lng_ref, lnb_ref, o_ref,
                    xn_ref, cur, prev, zc_ref, *, width, tiles_per_seq, nchunk):
    s = pl.program_id(0)
    bm, d = x_ref.shape
    rb = zc_ref.shape[0]

    @pl.when(s == 0)
    def _():
        cur[...] = jnp.zeros_like(cur)
        prev[...] = jnp.zeros_like(prev)

    starts_sequence = ((s - 1) % tiles_per_seq) == 0
    prev[0:HALO, :] = jnp.where(starts_sequence, 0.0, prev[bm:bm + HALO, :])
    prev[HALO:HALO + bm, :] = cur[...]
    xn_ref[...] = _rms(x_ref[...], g_ref[...]).astype(BF16)

    base = HALO - (width - 1)

    def conv_block(r):
        for c0 in range(0, d, LANES):
            cols = slice(c0, c0 + LANES)
            out = jnp.broadcast_to(dwb_ref[:, cols], (rb, LANES))
            for ph in range(SUBLANES):
                part = None
                for j in range(width):
                    if (base + j) % SUBLANES != ph:
                        continue
                    lo = r + base + j - ph
                    win = prev[lo:lo + rb + (SUBLANES if ph else 0), cols]
                    term = win * dw_ref[j:j + 1, cols]
                    part = term if part is None else part + term
                if part is not None:
                    if ph:
                        part = pltpu.roll(part, rb + SUBLANES - ph, axis=0)
                    out = out + part[:rb]
            zc_ref[:, cols] = out
        for r0 in range(0, rb, BF16_ROWS):
            acc = zc_ref[r0:r0 + BF16_ROWS, :]
            mu = jnp.mean(acc, axis=-1, keepdims=True)
            xc = acc - mu
            y = xc * lax.rsqrt(jnp.mean(xc * xc, axis=-1, keepdims=True) + EPS)
            y = _silu(y * lng_ref[...] + lnb_ref[...])
            o_ref[r + r0:r + r0 + BF16_ROWS, :] = y.astype(o_ref.dtype)

    cw = d // nchunk
    blocks_per_chunk = (bm // rb) // nchunk
    for c in range(nchunk):
        lin = slice(c * cw, (c + 1) * cw)
        gate = slice(d + c * cw, d + (c + 1) * cw)
        a = jnp.dot(xn_ref[...], w_ref[:, lin], preferred_element_type=F32) + b_ref[:, lin]
        b = jnp.dot(xn_ref[...], w_ref[:, gate], preferred_element_type=F32) + b_ref[:, gate]
        cur[:, lin] = a * jax.nn.sigmoid(b)
        for blk in range(c * blocks_per_chunk, (c + 1) * blocks_per_chunk):
            conv_block(blk * rb)


def _gluconv(x2, gain, pw1, pw1_b, dw, dw_b, ln_g, ln_b, seq, bm, nchunk, rb):
    m, d = x2.shape
    width = dw.shape[0]
    n_tiles = m // bm
    row = lambda a: a.reshape(1, -1)
    const = lambda shape: pl.BlockSpec(shape, lambda s: (0, 0))
    return pl.pallas_call(
        functools.partial(_gluconv_kernel, width=width, tiles_per_seq=seq // bm, nchunk=nchunk),
        grid=(n_tiles + 1,),
        in_specs=[
            pl.BlockSpec((bm, d), lambda s: (jnp.minimum(s, n_tiles - 1), 0)),
            const((1, d)),
            pl.BlockSpec((d, 2 * d), lambda s: (0, 0), pipeline_mode=pl.Buffered(1)),
            const((1, 2 * d)), const((width, d)), const((1, d)), const((1, d)), const((1, d)),
        ],
        out_specs=pl.BlockSpec((bm, d), lambda s: (jnp.maximum(s - 1, 0), 0)),
        out_shape=jax.ShapeDtypeStruct((m, d), BF16),
        scratch_shapes=[pltpu.VMEM((bm, d), BF16), pltpu.VMEM((bm, d), F32), pltpu.VMEM((HALO + bm, d), F32),
                        pltpu.VMEM((rb, d), F32)],
        compiler_params=_cparams(1),
        name="gluconv",
    )(x2, row(gain), pw1, row(pw1_b), dw, row(dw_b), row(ln_g), row(ln_b))


def _pw2_kernel(x_ref, z_ref, w_ref, b_ref, o_ref):
    o_ref[...] = x_ref[...] + jnp.dot(z_ref[...], w_ref[...], preferred_element_type=F32) + b_ref[...]


def _pw2(x2, zs, pw2, pw2_b, bm):
    m, d = x2.shape
    return pl.pallas_call(
        _pw2_kernel,
        grid=(m // bm,),
        in_specs=[
            pl.BlockSpec((bm, d), lambda i: (i, 0)),
            pl.BlockSpec((bm, d), lambda i: (i, 0)),
            pl.BlockSpec((d, d), lambda i: (0, 0)),
            pl.BlockSpec((1, d), lambda i: (0, 0)),
        ],
        out_specs=pl.BlockSpec((bm, d), lambda i: (i, 0)),
        out_shape=jax.ShapeDtypeStruct((m, d), F32),
        compiler_params=_cparams(1),
        name="pw2",
    )(x2, zs, pw2, pw2_b.reshape(1, d))


def _tiles(rows, seq, d, d_ff, in_cols, heads):
    return dict(
        inproj=dict(bm=min(1024, rows), bn=min(1024, in_cols)),
        gdn=dict(hg=min(8, heads), ts=min(512, seq)),
        gmlp=dict(tb=min(512, rows)),
        outproj=dict(bm=min(512, rows), bn=d),
        mlp=dict(bm=min(512, rows), bf=min(2048, d_ff)),
        gluconv=dict(bm=min(512, seq), nchunk=max(1, d // MXU_COLS), rb=CHUNK),
        pw2=dict(bm=min(512, rows)),
    )


def _even_layer(x2, bsz, seq, tiles, norm_g, w_in, conv_w, a_log, dt_bias, o_norm, ln_g, ln_b, w_s, b_s, w_out):
    heads = a_log.shape[0]
    groups = w_s.shape[0]
    qkvz = 4 * heads * LANES
    gate0 = qkvz
    uv0 = gate0 + 2 * heads
    w_bf = w_in.astype(BF16)
    w_ba = jnp.pad(w_bf[:, gate0:uv0], ((0, 0), (0, LANES - 2 * heads)))
    p2, ba2 = _inproj(x2, norm_g, w_bf, qkvz, w_bf[:, uv0:], w_ba, **tiles["inproj"])
    p3 = p2.reshape(bsz, seq, -1)
    out_a = _gdn(p3, ba2.reshape(bsz, seq, LANES), conv_w, a_log, dt_bias, o_norm, heads, **tiles["gdn"])
    gw = groups * LANES
    bs_b = jnp.broadcast_to(b_s[:, :, None], b_s.shape + (LANES,))
    out_b = _gmlp(p2, qkvz // gw, qkvz // gw + 1, ln_g, ln_b, w_s, bs_b, **tiles["gmlp"])
    return _outproj(x2, out_a.reshape(bsz * seq, -1), out_b, w_out.astype(BF16), **tiles["outproj"])


def _odd_layer(x2, seq, tiles, norm_g, pw1, pw1_b, dw, dw_b, ln_g, ln_b, pw2, pw2_b):
    zs = _gluconv(x2, norm_g, pw1.astype(BF16), pw1_b, dw, dw_b, ln_g, ln_b, seq, **tiles["gluconv"])
    return _pw2(x2, zs, pw2.astype(BF16), pw2_b, **tiles["pw2"])


def kernel(x, e_norm, e_w_in, e_conv_w, e_a_log, e_dt_bias, e_o_norm, e_ln_g, e_ln_b, e_w_s, e_b_s, e_w_out, o_norm, o_pw1, o_pw1_b, o_dw, o_dw_b, o_ln_g, o_ln_b, o_pw2, o_pw2_b, f_norm, f_w1, f_w2, final_norm):
    bsz, seq, d = x.shape
    depth = f_norm.shape[0]
    x2 = x.reshape(bsz * seq, d)
    heads = e_a_log.shape[1]
    tiles = _tiles(bsz * seq, seq, d, f_w1.shape[2], e_w_in.shape[2] - 2 * heads, heads)
    w1_bf = f_w1.astype(BF16)
    w2_bf = f_w2.astype(BF16)
    for layer in range(depth):
        i = layer // 2
        if layer % 2 == 0:
            x2 = _even_layer(x2, bsz, seq, tiles, e_norm[i], e_w_in[i], e_conv_w[i], e_a_log[i], e_dt_bias[i],
                             e_o_norm[i], e_ln_g[i], e_ln_b[i], e_w_s[i], e_b_s[i], e_w_out[i])
        else:
            x2 = _odd_layer(x2, seq, tiles, o_norm[i], o_pw1[i], o_pw1_b[i], o_dw[i], o_dw_b[i], o_ln_g[i],
                            o_ln_b[i], o_pw2[i], o_pw2_b[i])
        x2 = _mlp(x2, f_norm[layer], w1_bf, w2_bf, layer, final_norm,
                  final_norm=(layer == depth - 1), **tiles["mlp"])
    return x2.reshape(bsz, seq, d)
```
